```python
import math
import jax
import jax.numpy as jnp
from jax import lax
import numpy as np

D_MODEL = 1024
BATCH = 2
SEQ = 8192
DEPTH = 4

NSA_HEADS = 8
NSA_KV_HEADS = 2
NSA_GROUP = NSA_HEADS // NSA_KV_HEADS
NSA_HEAD_DIM = 64
CMP_BLOCK = 32
CMP_STRIDE = 16
CMP_HIDDEN = 128
SLC_BLOCK = 64
N_SELECTED = 16
WINDOW = 512
Q_BLOCK = 128
GDN_HEADS = 4
GDN_HEAD_DIM = 128
GDN_CHUNK = 64
CONV_WIDTH = 4
D_FF = 2816
PLE_DIM = 256
ROPE_THETA = 10000.0
EPS = 1e-6
FORCE_SCORE = 1e6
NEG_INF = -1e30

NSA_WIDTH = NSA_HEADS * NSA_HEAD_DIM
NSA_KV_WIDTH = NSA_KV_HEADS * NSA_HEAD_DIM
GDN_WIDTH = GDN_HEADS * GDN_HEAD_DIM
D_MIX = NSA_WIDTH + GDN_WIDTH
IN_SIZES = (NSA_WIDTH, NSA_KV_WIDTH, NSA_KV_WIDTH, NSA_KV_WIDTH, NSA_KV_WIDTH,
            NSA_KV_WIDTH, NSA_KV_WIDTH, 3 * NSA_HEADS, 3 * GDN_WIDTH, GDN_WIDTH,
            GDN_HEADS, GDN_HEADS)
D_IN = sum(IN_SIZES)

kernel_name = 'hybrid_nsa_gdn_macaron_ple'


def rmsnorm(x, w):
    xf = x.astype(jnp.float32)
    y = xf * lax.rsqrt(jnp.mean(xf * xf, axis=-1, keepdims=True) + EPS)
    return (y * w.astype(jnp.float32)).astype(x.dtype)


def l2norm(x):
    return x * lax.rsqrt(jnp.sum(x * x, axis=-1, keepdims=True) + EPS)


def swiglu(h, w1, w3, w2):
    return (jax.nn.silu(h @ w1) * (h @ w3)) @ w2


def rope_tables(seq, dim):
    inv = 1.0 / (ROPE_THETA ** (jnp.arange(0, dim, 2, dtype=jnp.float32) / dim))
    ang = jnp.arange(seq, dtype=jnp.float32)[:, None] * inv[None, :]
    ang = jnp.concatenate([ang, ang], axis=-1)
    return jnp.cos(ang), jnp.sin(ang)


def apply_rope(x, cos, sin):
    half = x.shape[-1] // 2
    rot = jnp.concatenate([-x[..., half:], x[..., :half]], axis=-1)
    return x * cos[None, :, None, :].astype(x.dtype) + rot * sin[None, :, None, :].astype(x.dtype)


def masked_softmax(s, mask):
    s = jnp.where(mask, s.astype(jnp.float32), NEG_INF)
    return jnp.where(mask, jax.nn.softmax(s, axis=-1), 0.0)


def causal_conv(x, w):
    c = x.shape[-1]
    return lax.conv_general_dilated(
        x, w[:, None, :].astype(x.dtype), window_strides=(1,),
        padding=((CONV_WIDTH - 1, 0),), dimension_numbers=('NWC', 'WIO', 'NWC'),
        feature_group_count=c)


def compress(kv, pe, w1, w2):
    b, s, hkv, dh = kv.shape
    n_cmp = (s - CMP_BLOCK) // CMP_STRIDE + 1
    idx = jnp.arange(n_cmp)[:, None] * CMP_STRIDE + jnp.arange(CMP_BLOCK)[None, :]
    blk = kv[:, idx] + pe[None, None, :, None, :]
    flat = blk.transpose(0, 1, 3, 2, 4).reshape(b, n_cmp, hkv, CMP_BLOCK * dh)
    return jax.nn.silu(flat @ w1) @ w2


def nsa_attention(q, k_cmp, v_cmp, ks, vs, kw, vw, gates):
    b, s, _, dh = q.shape
    n_cmp = k_cmp.shape[1]
    n_slc = s // SLC_BLOCK
    n_sel = min(N_SELECTED, n_slc)
    n_qb = s // Q_BLOCK
    scale = dh ** -0.5
    cmp_end = jnp.arange(n_cmp) * CMP_STRIDE + CMP_BLOCK - 1
    jc = jnp.arange(n_cmp)[:, None]
    js = jnp.arange(n_slc)[None, :]
    overlap = ((jc * CMP_STRIDE < (js + 1) * SLC_BLOCK)
               & (jc * CMP_STRIDE + CMP_BLOCK > js * SLC_BLOCK)).astype(jnp.float32)
    blk_ids = jnp.arange(n_slc)
    ks_blk = ks.reshape(b, n_slc, SLC_BLOCK, NSA_KV_HEADS, dh).transpose(0, 3, 1, 2, 4)
    vs_blk = vs.reshape(b, n_slc, SLC_BLOCK, NSA_KV_HEADS, dh).transpose(0, 3, 1, 2, 4)
    kw_pad = jnp.pad(kw, ((0, 0), (WINDOW, 0), (0, 0), (0, 0)))
    vw_pad = jnp.pad(vw, ((0, 0), (WINDOW, 0), (0, 0), (0, 0)))
    q_all = (q * scale).reshape(b, n_qb, Q_BLOCK, NSA_KV_HEADS, NSA_GROUP, dh).transpose(1, 0, 2, 3, 4, 5)
    g_all = gates.reshape(b, n_qb, Q_BLOCK, NSA_KV_HEADS, NSA_GROUP, 3).transpose(1, 0, 2, 3, 4, 5)
    gather = jax.vmap(jax.vmap(lambda blocks, ids: blocks[ids]))

    def one_block(args):
        bi, qb, gb = args
        t = bi * Q_BLOCK + jnp.arange(Q_BLOCK)
        s_c = jnp.einsum('bqhgd,bnhd->bhgqn', qb, k_cmp)
        p_cmp = masked_softmax(s_c, cmp_end[None, :] <= t[:, None])
        o_cmp = jnp.einsum('bhgqn,bnhd->bqhgd', p_cmp.astype(v_cmp.dtype), v_cmp)
        imp = jnp.einsum('bhgqn,ns->bhqs', p_cmp, overlap)
        cur = t // SLC_BLOCK
        forced = ((blk_ids[None, :] == 0) | (blk_ids[None, :] == cur[:, None])
                  | (blk_ids[None, :] == cur[:, None] - 1))
        valid = blk_ids[None, :] * SLC_BLOCK <= t[:, None]
        score = jnp.where(forced, FORCE_SCORE, jnp.where(valid, imp, -1.0))
        _, sel = lax.top_k(score, n_sel)
        k_sel = gather(ks_blk, sel)
        v_sel = gather(vs_blk, sel)
        pos = sel[..., None] * SLC_BLOCK + jnp.arange(SLC_BLOCK)
        m_s = (pos <= t[None, None, :, None, None]).reshape(b, NSA_KV_HEADS, 1, Q_BLOCK, n_sel * SLC_BLOCK)
        s_s = jnp.einsum('bqhgd,bhqnld->bhgqnl', qb, k_sel).reshape(
            b, NSA_KV_HEADS, NSA_GROUP, Q_BLOCK, n_sel * SLC_BLOCK)
        p_s = masked_softmax(s_s, m_s).reshape(b, NSA_KV_HEADS, NSA_GROUP, Q_BLOCK, n_sel, SLC_BLOCK)
        o_slc = jnp.einsum('bhgqnl,bhqnld->bqhgd', p_s.astype(v_sel.dtype), v_sel)
        kwb = lax.dynamic_slice_in_dim(kw_pad, bi * Q_BLOCK, Q_BLOCK + WINDOW, axis=1)
        vwb = lax.dynamic_slice_in_dim(vw_pad, bi * Q_BLOCK, Q_BLOCK + WINDOW, axis=1)
        kpos = bi * Q_BLOCK - WINDOW + jnp.arange(Q_BLOCK + WINDOW)
        dist = t[:, None] - kpos[None, :]
        m_w = (kpos[None, :] >= 0) & (dist >= 0) & (dist < WINDOW)
        s_w = jnp.einsum('bqhgd,bkhd->bhgqk', qb, kwb)
        p_w = masked_softmax(s_w, m_w)
        o_win = jnp.einsum('bhgqk,bkhd->bqhgd', p_w.astype(vwb.dtype), vwb)
        return gb[..., 0:1] * o_cmp + gb[..., 1:2] * o_slc + gb[..., 2:3] * o_win

    out = lax.map(one_block, (jnp.arange(n_qb), q_all, g_all))
    return out.transpose(1, 0, 2, 3, 4, 5).reshape(b, s, NSA_WIDTH)


def gated_delta_rule(q, k, v, g, beta):
    b, h, s, dk = q.shape
    dv = v.shape[-1]
    c = GDN_CHUNK
    n = s // c
    q = q * dk ** -0.5
    q, k, v = [t.reshape(b, h, n, c, t.shape[-1]) for t in (q, k, v)]
    gc = jnp.cumsum(g.reshape(b, h, n, c), axis=-1)
    beta = beta.reshape(b, h, n, c)
    ii = jnp.arange(c)[:, None]
    jj = jnp.arange(c)[None, :]
    incl = ii >= jj
    strict = ii > jj
    decay = jnp.exp(jnp.where(incl, gc[..., :, None] - gc[..., None, :], -jnp.inf))
    kb = k * beta[..., None]
    a_s = jnp.where(strict, jnp.einsum('bhnik,bhnjk->bhnij', kb, k) * decay, 0.0)
    rhs = jnp.concatenate([v * beta[..., None], kb * jnp.exp(gc)[..., None]], axis=-1)
    sol = lax.linalg.triangular_solve(a_s, rhs, left_side=True, lower=True, unit_diagonal=True)
    u, w = sol[..., :dv], sol[..., dv:]
    attn = jnp.where(incl, jnp.einsum('bhnik,bhnjk->bhnij', q, k) * decay, 0.0)
    q_dec = q * jnp.exp(gc)[..., None]
    g_last = gc[..., -1]
    k_dec = k * jnp.exp(g_last[..., None] - gc)[..., None]
    xs = tuple(jnp.moveaxis(t, 2, 0) for t in (u, w, attn, q_dec, k_dec, g_last))

    def step(state, inp):
        u_n, w_n, a_n, qd, kd, gl = inp
        v_new = u_n - jnp.einsum('bhck,bhkv->bhcv', w_n, state)
        o = jnp.einsum('bhck,bhkv->bhcv', qd, state) + jnp.einsum('bhij,bhjv->bhiv', a_n, v_new)
        state = state * jnp.exp(gl)[..., None, None] + jnp.einsum('bhck,bhcv->bhkv', kd, v_new)
        return state, o

    _, o = lax.scan(step, jnp.zeros((b, h, dk, dv), jnp.float32), xs)
    return jnp.moveaxis(o, 0, 2).reshape(b, h, s, dv)


def hybrid_mixer(h, w_in, cmp_pe_k, cmp_pe_v, cmp_k_w1, cmp_k_w2, cmp_v_w1, cmp_v_w2,
                 gdn_conv, gdn_a_log, gdn_dt_bias, gdn_norm, w_out, cos, sin):
    b, s, _ = h.shape
    z = h @ w_in
    (q, kc, vc, ks, vs, kw, vw, g_nsa, qkv, zg, a_in, b_in) = jnp.split(
        z, np.cumsum(IN_SIZES)[:-1].tolist(), axis=-1)
    q = apply_rope(q.reshape(b, s, NSA_HEADS, NSA_HEAD_DIM), cos, sin)
    kc, ks, kw = [apply_rope(t.reshape(b, s, NSA_KV_HEADS, NSA_HEAD_DIM), cos, sin) for t in (kc, ks, kw)]
    vc, vs, vw = [t.reshape(b, s, NSA_KV_HEADS, NSA_HEAD_DIM) for t in (vc, vs, vw)]
    k_cmp = compress(kc, cmp_pe_k, cmp_k_w1, cmp_k_w2)
    v_cmp = compress(vc, cmp_pe_v, cmp_v_w1, cmp_v_w2)
    gates = jax.nn.sigmoid(g_nsa).reshape(b, s, NSA_HEADS, 3)
    o_nsa = nsa_attention(q, k_cmp, v_cmp, ks, vs, kw, vw, gates)
    qkv = jax.nn.silu(causal_conv(qkv, gdn_conv)).astype(jnp.float32)
    gq, gk, gv = jnp.split(qkv, 3, axis=-1)
    to_heads = lambda t: t.reshape(b, s, GDN_HEADS, GDN_HEAD_DIM).transpose(0, 2, 1, 3)
    gq, gk, gv = l2norm(to_heads(gq)), l2norm(to_heads(gk)), to_heads(gv)
    log_decay = -jnp.exp(gdn_a_log.astype(jnp.float32)) * jax.nn.softplus(
        a_in.astype(jnp.float32) + gdn_dt_bias.astype(jnp.float32))
    beta = jax.nn.sigmoid(b_in.astype(jnp.float32))
    o = gated_delta_rule(gq, gk, gv, log_decay.transpose(0, 2, 1), beta.transpose(0, 2, 1))
    o = o.transpose(0, 2, 1, 3)
    o = rmsnorm(o, gdn_norm) * jax.nn.silu(zg.reshape(b, s, GDN_HEADS, GDN_HEAD_DIM).astype(jnp.float32))
    o_gdn = o.reshape(b, s, GDN_WIDTH).astype(h.dtype)
    return jnp.concatenate([o_nsa, o_gdn], axis=-1) @ w_out


def setup_inputs(seed: int = 0) -> dict:
    key = jax.random.key(seed)
    k = jax.random.split(key, 28)
    L = DEPTH
    nrm = lambda kk, shape, fan: jax.random.normal(kk, shape, jnp.float32) * fan ** -0.5
    gain = lambda kk, shape: 1.0 + 0.02 * jax.random.normal(kk, shape, jnp.float32)
    dt = jnp.exp(jax.random.uniform(k[16], (L, GDN_HEADS), jnp.float32,
                                    minval=math.log(1e-3), maxval=math.log(1e-1)))
    return {
        'x': jax.random.normal(k[0], (BATCH, SEQ, D_MODEL), jnp.float32),
        'p': jax.random.normal(k[1], (DEPTH, BATCH, SEQ, PLE_DIM), jnp.float32),
        'ffn1_norm': gain(k[2], (L, D_MODEL)),
        'ffn1_w1': nrm(k[3], (L, D_MODEL, D_FF), D_MODEL),
        'ffn1_w3': nrm(k[4], (L, D_MODEL, D_FF), D_MODEL),
        'ffn1_w2': nrm(k[5], (L, D_FF, D_MODEL), D_FF),
        'mix_norm': gain(k[6], (L, D_MODEL)),
        'w_in': nrm(k[7], (L, D_MODEL, D_IN), D_MODEL),
        'cmp_pe_k': 0.02 * jax.random.normal(k[8], (L, CMP_BLOCK, NSA_HEAD_DIM), jnp.float32),
        'cmp_pe_v': 0.02 * jax.random.normal(k[9], (L, CMP_BLOCK, NSA_HEAD_DIM), jnp.float32),
        'cmp_k_w1': nrm(k[10], (L, CMP_BLOCK * NSA_HEAD_DIM, CMP_HIDDEN), CMP_BLOCK * NSA_HEAD_DIM),
        'cmp_k_w2': nrm(k[11], (L, CMP_HIDDEN, NSA_HEAD_DIM), CMP_HIDDEN),
        'cmp_v_w1': nrm(k[12], (L, CMP_BLOCK * NSA_HEAD_DIM, CMP_HIDDEN), CMP_BLOCK * NSA_HEAD_DIM),
        'cmp_v_w2': nrm(k[13], (L, CMP_HIDDEN, NSA_HEAD_DIM), CMP_HIDDEN),
        'gdn_conv': nrm(k[14], (L, CONV_WIDTH, 3 * GDN_WIDTH), CONV_WIDTH),
        'gdn_a_log': jnp.log(jax.random.uniform(k[15], (L, GDN_HEADS), jnp.float32, minval=1.0, maxval=16.0)),
        'gdn_dt_bias': dt + jnp.log(-jnp.expm1(-dt)),
        'gdn_norm': gain(k[17], (L, GDN_HEAD_DIM)),
        'w_out': nrm(k[18], (L, D_MIX, D_MODEL), D_MIX),
        'ffn2_norm': gain(k[19], (L, D_MODEL)),
        'ffn2_w1': nrm(k[20], (L, D_MODEL, D_FF), D_MODEL),
        'ffn2_w3': nrm(k[21], (L, D_MODEL, D_FF), D_MODEL),
        'ffn2_w2': nrm(k[22], (L, D_FF, D_MODEL), D_FF),
        'ple_norm': gain(k[23], (L, D_MODEL)),
        'ple_gate': nrm(k[24], (L, D_MODEL, D_MODEL), D_MODEL),
        'ple_proj': nrm(k[25], (L, PLE_DIM, D_MODEL), PLE_DIM),
        'final_norm': gain(k[26], (D_MODEL,)),
    }


def reference(x, p, ffn1_norm, ffn1_w1, ffn1_w3, ffn1_w2, mix_norm, w_in, cmp_pe_k, cmp_pe_v,
              cmp_k_w1, cmp_k_w2, cmp_v_w1, cmp_v_w2, gdn_conv, gdn_a_log, gdn_dt_bias, gdn_norm,
              w_out, ffn2_norm, ffn2_w1, ffn2_w3, ffn2_w2, ple_norm, ple_gate, ple_proj, final_norm):
    cos, sin = rope_tables(x.shape[1], NSA_HEAD_DIM)
    for i in range(DEPTH):
        x = x + 0.5 * swiglu(rmsnorm(x, ffn1_norm[i]), ffn1_w1[i], ffn1_w3[i], ffn1_w2[i])
        x = x + hybrid_mixer(rmsnorm(x, mix_norm[i]), w_in[i], cmp_pe_k[i], cmp_pe_v[i],
                             cmp_k_w1[i], cmp_k_w2[i], cmp_v_w1[i], cmp_v_w2[i], gdn_conv[i],
                             gdn_a_log[i], gdn_dt_bias[i], gdn_norm[i], w_out[i], cos, sin)
        x = x + 0.5 * swiglu(rmsnorm(x, ffn2_norm[i]), ffn2_w1[i], ffn2_w3[i], ffn2_w2[i])
        gate = jax.nn.sigmoid(rmsnorm(x, ple_norm[i]) @ ple_gate[i])
        x = x + gate * (p[i] @ ple_proj[i])
    return rmsnorm(x, final_norm)
```

```python
import functools

import jax
import jax.numpy as jnp
from jax import lax
from jax.experimental import pallas as pl
from jax.experimental.pallas import tpu as pltpu

F32 = jnp.float32
BF16 = jnp.bfloat16

D_MODEL = 1024
NSA_HEADS = 8
NSA_KV_HEADS = 2
NSA_GROUP = NSA_HEADS // NSA_KV_HEADS
NSA_HEAD_DIM = 64
CMP_BLOCK = 32
CMP_STRIDE = 16
CMP_HIDDEN = 128
SLC_BLOCK = 64
N_SELECTED = 16
WINDOW = 512
Q_BLOCK = 128
GDN_HEADS = 4
GDN_HEAD_DIM = 128
GDN_CHUNK = 64
CONV_WIDTH = 4
D_FF = 2816
PLE_DIM = 256
ROPE_THETA = 10000.0
EPS = 1e-6
FORCE_SCORE = 1e6
NEG_INF = -1e30

NSA_WIDTH = NSA_HEADS * NSA_HEAD_DIM
NSA_KV_WIDTH = NSA_KV_HEADS * NSA_HEAD_DIM
GDN_WIDTH = GDN_HEADS * GDN_HEAD_DIM
IN_SIZES = (NSA_WIDTH, NSA_KV_WIDTH, NSA_KV_WIDTH, NSA_KV_WIDTH, NSA_KV_WIDTH,
            NSA_KV_WIDTH, NSA_KV_WIDTH, 3 * NSA_HEADS, 3 * GDN_WIDTH, GDN_WIDTH,
            GDN_HEADS, GDN_HEADS)

LANES = 128
NSA_MAIN = NSA_WIDTH + 6 * NSA_KV_WIDTH
GDN_MAIN = 4 * GDN_WIDTH
SMALL_GATE = 3 * NSA_HEADS
SMALL_A = SMALL_GATE
SMALL_B = SMALL_GATE + GDN_HEADS
W_IN_PACKED = NSA_MAIN + GDN_MAIN + LANES
SEL_BIAS = -2.0 ** 100
M_INIT = -3.0e38
VMEM_LIMIT = 56 * 1024 * 1024


def _cparams(sem):
    return pltpu.CompilerParams(dimension_semantics=sem, vmem_limit_bytes=VMEM_LIMIT)


def _rms(x, w):
    ms = jnp.mean(x * x, axis=-1, keepdims=True)
    return x * lax.rsqrt(ms + EPS) * w


def _sigmoid(x):
    return 1.0 / (1.0 + jnp.exp(-x))


def _silu(x):
    return x * _sigmoid(x)


def _dot(a, b):
    return jnp.dot(a, b, preferred_element_type=F32)


def _dot_nt(a, b):
    return lax.dot_general(a, b, (((1,), (1,)), ((), ())), preferred_element_type=F32)


def _split3(x):
    hi = x.astype(BF16)
    r = x - hi.astype(F32)
    mid = r.astype(BF16)
    lo = (r - mid.astype(F32)).astype(BF16)
    return hi, mid, lo


def _dot_exact_lhs(a_bf, x):
    hi, mid, lo = _split3(x)
    return _dot(a_bf, hi) + (_dot(a_bf, mid) + _dot(a_bf, lo))


def _dot3(a, b):
    ah, am, al = _split3(a)
    bh, bm, bl = _split3(b)
    small = _dot(ah, bl) + _dot(al, bh) + _dot(am, bm)
    mid = _dot(ah, bm) + _dot(am, bh)
    return _dot(ah, bh) + (mid + small)


def _ffn_body(x_ref, nw_ref, w1_ref, w3_ref, w2_ref, o_ref, h_ref, acc_ref):
    j = pl.program_id(1)

    @pl.when(j == 0)
    def _():
        h_ref[...] = _rms(x_ref[...], nw_ref[...]).astype(BF16)
        acc_ref[...] = jnp.zeros_like(acc_ref)

    h = h_ref[...]
    u = _dot(h, w1_ref[...])
    g = _dot(h, w3_ref[...])
    a = (_silu(u) * g).astype(BF16)
    acc_ref[...] += _dot(a, w2_ref[...])

    @pl.when(j == pl.num_programs(1) - 1)
    def _():
        o_ref[...] = x_ref[...] + 0.5 * acc_ref[...]


def _ffn(x, nw, w1, w3, w2, *, tm, tf):
    t, d = x.shape
    ff = w1.shape[1]
    return pl.pallas_call(
        _ffn_body,
        grid=(t // tm, ff // tf),
        in_specs=[
            pl.BlockSpec((tm, d), lambda i, j: (i, 0)),
            pl.BlockSpec((1, d), lambda i, j: (0, 0)),
            pl.BlockSpec((d, tf), lambda i, j: (0, j)),
            pl.BlockSpec((d, tf), lambda i, j: (0, j)),
            pl.BlockSpec((tf, d), lambda i, j: (j, 0)),
        ],
        out_specs=pl.BlockSpec((tm, d), lambda i, j: (i, 0)),
        out_shape=jax.ShapeDtypeStruct((t, d), F32),
        scratch_shapes=[pltpu.VMEM((tm, d), BF16), pltpu.VMEM((tm, d), F32)],
        compiler_params=_cparams(("parallel", "arbitrary")),
        name="ffn",
    )(x, nw, w1, w3, w2)


def _rope(xg, cos, sin_signed, first_half):
    fwd = pltpu.roll(xg, LANES - NSA_HEAD_DIM // 2, 1)
    bwd = pltpu.roll(xg, NSA_HEAD_DIM // 2, 1)
    return xg * cos + jnp.where(first_half, fwd, bwd) * sin_signed


def _inproj_body(x_ref, nw_ref, w_ref, cos_ref, sin_ref,
                 q_ref, kv_ref, cmpf_ref, qkv_ref, zg_ref, small_ref):
    h = _rms(x_ref[...], nw_ref[...]).astype(BF16)
    cos = cos_ref[...]
    sin_s = sin_ref[...]
    lane = lax.broadcasted_iota(jnp.int32, (1, LANES), 1)
    first_half = (lane & (NSA_HEAD_DIM - 1)) < (NSA_HEAD_DIM // 2)

    z = _dot(h, w_ref[:, 0:NSA_MAIN])
    scale = NSA_HEAD_DIM ** -0.5
    for c in range(NSA_WIDTH // LANES):
        zq = _rope(z[:, c * LANES:(c + 1) * LANES], cos, sin_s, first_half)
        q_ref[:, c * LANES:(c + 1) * LANES] = (zq * scale).astype(BF16)
    base = NSA_WIDTH
    for c in range(6):
        zc = z[:, base + c * LANES: base + (c + 1) * LANES]
        if c % 2 == 0:
            zc = _rope(zc, cos, sin_s, first_half)
        kv_ref[:, c * LANES:(c + 1) * LANES] = zc.astype(BF16)
        if c < 2:
            cmpf_ref[:, c * LANES:(c + 1) * LANES] = zc

    zg = _dot(h, w_ref[:, NSA_MAIN:NSA_MAIN + GDN_MAIN])
    qkv_ref[...] = zg[:, 0:3 * GDN_WIDTH]
    zg_ref[...] = zg[:, 3 * GDN_WIDTH:]

    zs = _dot(h, w_ref[:, NSA_MAIN + GDN_MAIN:])
    is_raw = (lane >= SMALL_A) & (lane < SMALL_B)
    small_ref[...] = jnp.where(is_raw, zs, _sigmoid(zs))


def _inproj(x, nw, w_packed, cos2, sin2, *, tm, seq):
    t, d = x.shape
    nseq = seq // tm
    row = lambda i: (i, 0)
    return pl.pallas_call(
        _inproj_body,
        grid=(t // tm,),
        in_specs=[
            pl.BlockSpec((tm, d), row),
            pl.BlockSpec((1, d), lambda i: (0, 0)),
            pl.BlockSpec((d, W_IN_PACKED), lambda i: (0, 0)),
            pl.BlockSpec((tm, LANES), lambda i: (i % nseq, 0)),
            pl.BlockSpec((tm, LANES), lambda i: (i % nseq, 0)),
        ],
        out_specs=[
            pl.BlockSpec((tm, NSA_WIDTH), row),
            pl.BlockSpec((tm, 6 * NSA_KV_WIDTH), row),
            pl.BlockSpec((tm, 2 * NSA_KV_WIDTH), row),
            pl.BlockSpec((tm, 3 * GDN_WIDTH), row),
            pl.BlockSpec((tm, GDN_WIDTH), row),
            pl.BlockSpec((tm, LANES), row),
        ],
        out_shape=[
            jax.ShapeDtypeStruct((t, NSA_WIDTH), BF16),
            jax.ShapeDtypeStruct((t, 6 * NSA_KV_WIDTH), BF16),
            jax.ShapeDtypeStruct((t, 2 * NSA_KV_WIDTH), F32),
            jax.ShapeDtypeStruct((t, 3 * GDN_WIDTH), F32),
            jax.ShapeDtypeStruct((t, GDN_WIDTH), F32),
            jax.ShapeDtypeStruct((t, LANES), F32),
        ],
        compiler_params=_cparams(("parallel",)),
        name="inproj",
    )(x, nw, w_packed, cos2, sin2)


def _compress_body(xk_ref, xv_ref, pek_ref, pev_ref, w1k_ref, w2k_ref, w1v_ref, w2v_ref,
                   ok_ref, ov_ref):
    half = CMP_STRIDE * NSA_HEAD_DIM

    def one(x_ref, pe_ref, w1_ref, w2_ref, o_ref):
        x = x_ref[0, 0]
        nh = x.shape[0]
        a = _dot((x + pe_ref[0:1, :]).astype(BF16), w1_ref[0:half, :])
        b = _dot((x + pe_ref[1:2, :]).astype(BF16), w1_ref[half:2 * half, :])
        hid = a + pltpu.roll(b, nh - 1, 0)
        o_ref[0, 0] = _dot(_silu(hid).astype(BF16), w2_ref[...]).astype(BF16)

    one(xk_ref, pek_ref, w1k_ref, w2k_ref, ok_ref)
    one(xv_ref, pev_ref, w1v_ref, w2v_ref, ov_ref)


def _compress(xk, xv, pek, pev, w1k, w2k, w1v, w2v):
    b, hkv, nh, wide = xk.shape
    xspec = pl.BlockSpec((1, 1, nh, wide), lambda i, j: (i, j, 0, 0))
    full = lambda a: pl.BlockSpec(a.shape, lambda i, j: (0,) * a.ndim)
    ospec = pl.BlockSpec((1, 1, nh, NSA_HEAD_DIM), lambda i, j: (i, j, 0, 0))
    oshape = jax.ShapeDtypeStruct((b, hkv, nh, NSA_HEAD_DIM), BF16)
    return pl.pallas_call(
        _compress_body,
        grid=(b, hkv),
        in_specs=[xspec, xspec, full(pek), full(pev), full(w1k), full(w2k), full(w1v), full(w2v)],
        out_specs=[ospec, ospec],
        out_shape=[oshape, oshape],
        compiler_params=_cparams(("parallel", "parallel")),
        name="compress",
    )(xk, xv, pek, pev, w1k, w2k, w1v, w2v)


def _nsa_body(q_ref, kc_ref, vc_ref, kaug_ref, vs_ref, kw_ref, vw_ref, g_ref, ovt_ref,
              o_ref, *, n_slc, n_sel, tk):
    i = pl.program_id(2)
    t0 = i * Q_BLOCK
    rows = NSA_GROUP * Q_BLOCK
    dh = NSA_HEAD_DIM
    qs = jnp.concatenate([q_ref[0, :, g * LANES:(g + 1) * LANES] for g in range(NSA_GROUP)],
                         axis=0)
    qs64 = qs[:, 0:dh]
    row = lax.broadcasted_iota(jnp.int32, (rows, 1), 0)
    t_rows = t0 + (row & (Q_BLOCK - 1))

    kc = kc_ref[0, 0]
    nc = kc.shape[0]
    s_c = _dot_nt(qs64, kc)
    ccol = lax.broadcasted_iota(jnp.int32, (1, nc), 1)
    mask_c = (ccol * CMP_STRIDE + (CMP_BLOCK - 1)) <= t_rows
    s_c = jnp.where(mask_c, s_c, NEG_INF)
    m_c = jnp.max(s_c, axis=1, keepdims=True)
    e_c = jnp.where(mask_c, jnp.exp(s_c - m_c), 0.0)
    l_c = jnp.sum(e_c, axis=1, keepdims=True)
    p_c = e_c * (1.0 / jnp.where(l_c > 0.0, l_c, 1.0))
    o_cmp = _dot(p_c.astype(BF16), vc_ref[0, 0])

    pg = p_c[0:Q_BLOCK]
    for g in range(1, NSA_GROUP):
        pg = pg + p_c[g * Q_BLOCK:(g + 1) * Q_BLOCK]
    ph, pm, plo = _split3(pg)
    ovt = ovt_ref[...]
    imp_t = _dot_nt(ovt, ph) + (_dot_nt(ovt, pm) + _dot_nt(ovt, plo))

    blk = lax.broadcasted_iota(jnp.int32, (LANES, Q_BLOCK), 0)
    tcol = t0 + lax.broadcasted_iota(jnp.int32, (LANES, Q_BLOCK), 1)
    cur = jnp.right_shift(tcol, SLC_BLOCK.bit_length() - 1)
    forced = (blk == 0) | (blk == cur) | (blk == cur - 1)
    valid = blk * SLC_BLOCK <= tcol
    score = jnp.where(forced, FORCE_SCORE, jnp.where(valid, imp_t, -1.0))
    score = jnp.where(blk < n_slc, score, -jnp.inf)
    bias_t = jnp.full((LANES, Q_BLOCK), SEL_BIAS, F32)
    for _ in range(n_sel):
        mx = jnp.max(score, axis=0, keepdims=True)
        idx = jnp.min(jnp.where(score == mx, blk, 2 * LANES), axis=0, keepdims=True)
        hit = blk == idx
        bias_t = jnp.where(hit, 0.0, bias_t)
        score = jnp.where(hit, -jnp.inf, score)
    bias = jnp.transpose(bias_t).astype(BF16)
    q_aug = jnp.concatenate([jnp.concatenate([bias] * NSA_GROUP, axis=0), qs], axis=1)

    def slc_step(j, carry, causal):
        m, l, acc = carry
        k0 = pl.multiple_of(j * tk, tk)
        s = _dot_nt(q_aug, kaug_ref[0, 0, pl.ds(k0, tk), :])
        if causal:
            kpos = k0 + lax.broadcasted_iota(jnp.int32, (1, tk), 1)
            s = jnp.where(kpos <= t_rows, s, NEG_INF)
        m_new = jnp.maximum(m, jnp.max(s, axis=1, keepdims=True))
        alpha = jnp.exp(m - m_new)
        p = jnp.exp(s - m_new)
        l = alpha * l + jnp.sum(p, axis=1, keepdims=True)
        acc = alpha * acc + _dot(p.astype(BF16), vs_ref[0, 0, pl.ds(k0, tk), :])
        return m_new, l, acc

    n_full = t0 // tk
    init = (jnp.full((rows, 1), M_INIT, F32), jnp.zeros((rows, 1), F32), jnp.zeros((rows, dh), F32))
    carry = lax.fori_loop(0, n_full, functools.partial(slc_step, causal=False), init)
    _, l_s, acc_s = slc_step(n_full, carry, True)
    o_slc = acc_s * (1.0 / l_s)

    kwt = kw_ref[0, 0, pl.ds(pl.multiple_of(t0, Q_BLOCK), Q_BLOCK + WINDOW), :]
    vwt = vw_ref[0, 0, pl.ds(pl.multiple_of(t0, Q_BLOCK), Q_BLOCK + WINDOW), :]
    s_w = _dot_nt(qs64, kwt)
    kpos = t0 - WINDOW + lax.broadcasted_iota(jnp.int32, (1, Q_BLOCK + WINDOW), 1)
    dist = t_rows - kpos
    mask_w = (kpos >= 0) & (dist >= 0) & (dist < WINDOW)
    s_w = jnp.where(mask_w, s_w, NEG_INF)
    m_w = jnp.max(s_w, axis=1, keepdims=True)
    e_w = jnp.where(mask_w, jnp.exp(s_w - m_w), 0.0)
    l_w = jnp.sum(e_w, axis=1, keepdims=True)
    o_win = _dot(e_w.astype(BF16), vwt) * (1.0 / l_w)

    g = g_ref[0, 0, 0]
    out = g[:, 0:1] * o_cmp + g[:, 1:2] * o_slc + g[:, 2:3] * o_win
    o_ref[0, 0, 0] = out.astype(BF16)


def _nsa(q_pad, kcmp, vcmp, kaug, vs, kw_pad, vw_pad, gates, ovt, *, tk):
    b, seq, _ = q_pad.shape
    hkv = NSA_KV_HEADS
    nqb = seq // Q_BLOCK
    rows = NSA_GROUP * Q_BLOCK
    nc = kcmp.shape[2]
    n_slc = seq // SLC_BLOCK
    per_head = lambda a: pl.BlockSpec((1, 1) + a.shape[2:], lambda bi, hi, qi: (bi, hi, 0, 0))
    body = functools.partial(_nsa_body, n_slc=n_slc, n_sel=min(N_SELECTED, n_slc), tk=tk)
    return pl.pallas_call(
        body,
        grid=(b, hkv, nqb),
        in_specs=[
            pl.BlockSpec((1, Q_BLOCK, NSA_GROUP * LANES), lambda bi, hi, qi: (bi, qi, hi)),
            per_head(kcmp), per_head(vcmp), per_head(kaug), per_head(vs),
            per_head(kw_pad), per_head(vw_pad),
            pl.BlockSpec((1, 1, 1, rows, 3), lambda bi, hi, qi: (bi, hi, qi, 0, 0)),
            pl.BlockSpec(ovt.shape, lambda bi, hi, qi: (0, 0)),
        ],
        out_specs=pl.BlockSpec((1, 1, 1, rows, NSA_HEAD_DIM), lambda bi, hi, qi: (bi, hi, qi, 0, 0)),
        out_shape=jax.ShapeDtypeStruct((b, hkv, nqb, rows, NSA_HEAD_DIM), BF16),
        compiler_params=_cparams(("parallel", "parallel", "arbitrary")),
        name="nsa",
    )(q_pad, kcmp, vcmp, kaug, vs, kw_pad, vw_pad, gates, ovt)


GDN_TILE = 2 * GDN_CHUNK


def _gdn_prep_body(qkv_ref, halo_ref, cw_ref, small_ref, alog_ref, dtb_ref,
                   u_ref, wq_ref, kdt_ref, attn_ref, eg_ref, *, ts):
    i = pl.program_id(1)
    x = qkv_ref[0]
    halo = jnp.where(i > 0, halo_ref[0], 0.0)
    xx = jnp.concatenate([halo, x], axis=0)
    y = x * cw_ref[CONV_WIDTH - 1:CONV_WIDTH, :]
    for d in range(1, CONV_WIDTH):
        shifted = pltpu.roll(xx, d, 0)[8:]
        y = y + shifted * cw_ref[CONV_WIDTH - 1 - d:CONV_WIDTH - d, :]
    y = _silu(y)

    sm = small_ref[0]
    sp_in = sm + dtb_ref[...]
    softplus = jnp.maximum(sp_in, 0.0) + jnp.log(1.0 + jnp.exp(-jnp.abs(sp_in)))
    glog = -jnp.exp(alog_ref[...]) * softplus

    ri = lax.broadcasted_iota(jnp.int32, (ts, ts), 0)
    ci = lax.broadcasted_iota(jnp.int32, (ts, ts), 1)
    sh = GDN_CHUNK.bit_length() - 1
    same = jnp.right_shift(ri, sh) == jnp.right_shift(ci, sh)
    tril = jnp.where(same & (ri >= ci), 1.0, 0.0).astype(BF16)
    ones = jnp.where(same, 1.0, 0.0).astype(BF16)
    gcum = _dot_exact_lhs(tril, glog)
    glast = _dot_exact_lhs(ones, glog)

    r2 = lax.broadcasted_iota(jnp.int32, (GDN_TILE, GDN_TILE), 0)
    c2 = lax.broadcasted_iota(jnp.int32, (GDN_TILE, GDN_TILE), 1)
    same2 = jnp.right_shift(r2, sh) == jnp.right_shift(c2, sh)
    incl = same2 & (r2 >= c2)
    strict = same2 & (r2 > c2)
    eye = jnp.where(r2 == c2, 1.0, 0.0)
    qscale = GDN_HEAD_DIM ** -0.5

    for c in range(ts // GDN_TILE):
        r0 = c * GDN_TILE
        gc_tile = gcum[r0:r0 + GDN_TILE]
        gc_rows = jnp.transpose(gc_tile)
        for h in range(GDN_HEADS):
            lo = h * GDN_HEAD_DIM
            qh = y[r0:r0 + GDN_TILE, lo:lo + GDN_HEAD_DIM]
            kh = y[r0:r0 + GDN_TILE, GDN_WIDTH + lo:GDN_WIDTH + lo + GDN_HEAD_DIM]
            vh = y[r0:r0 + GDN_TILE, 2 * GDN_WIDTH + lo:2 * GDN_WIDTH + lo + GDN_HEAD_DIM]
            qh = qh * lax.rsqrt(jnp.sum(qh * qh, axis=-1, keepdims=True) + EPS)
            kh = kh * lax.rsqrt(jnp.sum(kh * kh, axis=-1, keepdims=True) + EPS)
            gc_col = gc_tile[:, SMALL_A + h:SMALL_A + h + 1]
            gc_row = gc_rows[SMALL_A + h:SMALL_A + h + 1, :]
            gl_col = glast[r0:r0 + GDN_TILE, SMALL_A + h:SMALL_A + h + 1]
            beta = sm[r0:r0 + GDN_TILE, SMALL_B + h:SMALL_B + h + 1]

            decay = jnp.where(incl, jnp.exp(jnp.minimum(gc_col - gc_row, 0.0)), 0.0)
            kb = kh * beta
            k_bf = kh.astype(BF16)
            a_s = jnp.where(strict, _dot_nt(kb.astype(BF16), k_bf) * decay, 0.0)
            xinv = eye - a_s
            pw = _dot3(a_s, a_s)
            steps = GDN_CHUNK.bit_length() - 2
            for s in range(steps):
                xinv = xinv + _dot3(xinv, pw)
                if s + 1 < steps:
                    pw = _dot3(pw, pw)
            egc = jnp.exp(gc_col)
            u = _dot3(xinv, vh * beta)
            w = _dot3(xinv, kb * egc)
            qs = qh * qscale
            attn = jnp.where(incl, _dot_nt(qs.astype(BF16), k_bf) * decay, 0.0)
            q_dec = qs * egc
            k_dec = kh * jnp.exp(gl_col - gc_col)

            u_ref[0, h, r0:r0 + GDN_TILE, :] = u
            attn_ref[0, h, r0:r0 + GDN_TILE, :] = attn.astype(BF16)
            kdt_ref[0, h, r0:r0 + GDN_TILE, :] = jnp.transpose(k_dec).astype(BF16)
            for cc in range(2):
                a0 = cc * GDN_CHUNK
                wq = jnp.concatenate([w[a0:a0 + GDN_CHUNK], q_dec[a0:a0 + GDN_CHUNK]], axis=0)
                n0 = 2 * r0 + cc * GDN_TILE
                wq_ref[0, h, n0:n0 + GDN_TILE, :] = wq.astype(BF16)
                e0 = (r0 // GDN_CHUNK + cc) * 8
                eg_ref[0, h, e0:e0 + 8, :] = jnp.broadcast_to(
                    jnp.exp(gl_col[a0:a0 + 8]), (8, GDN_HEAD_DIM))


def _gdn_prep(qkv, cw, small, alog_row, dtb_row, *, ts):
    b, seq, wide = qkv.shape
    nt = seq // ts
    hd = GDN_HEAD_DIM
    hspec = lambda rows: pl.BlockSpec((1, GDN_HEADS, rows, hd), lambda bi, ti: (bi, 0, ti, 0))
    hshape = lambda rows, dt: jax.ShapeDtypeStruct((b, GDN_HEADS, rows, hd), dt)
    return pl.pallas_call(
        functools.partial(_gdn_prep_body, ts=ts),
        grid=(b, nt),
        in_specs=[
            pl.BlockSpec((1, ts, wide), lambda bi, ti: (bi, ti, 0)),
            pl.BlockSpec((1, 8, wide), lambda bi, ti: (bi, jnp.maximum(ti * (ts // 8) - 1, 0), 0)),
            pl.BlockSpec(cw.shape, lambda bi, ti: (0, 0)),
            pl.BlockSpec((1, ts, LANES), lambda bi, ti: (bi, ti, 0)),
            pl.BlockSpec((1, LANES), lambda bi, ti: (0, 0)),
            pl.BlockSpec((1, LANES), lambda bi, ti: (0, 0)),
        ],
        out_specs=[hspec(ts), hspec(2 * ts), hspec(ts), hspec(ts), hspec(ts // 8)],
        out_shape=[hshape(seq, F32), hshape(2 * seq, BF16), hshape(seq, BF16),
                   hshape(seq, BF16), hshape(seq // 8, F32)],
        compiler_params=_cparams(("parallel", "parallel")),
        name="gdn_prep",
    )(qkv, qkv, cw, small, alog_row, dtb_row)


def _gdn_scan_body(u_ref, wq_ref, kdt_ref, attn_ref, eg_ref, zg_ref, gn_ref, o_ref, st_ref, *, ts):
    @pl.when(pl.program_id(1) == 0)
    def _():
        st_ref[...] = jnp.zeros_like(st_ref)

    hd = GDN_HEAD_DIM
    gn = gn_ref[...]
    zeros = jnp.zeros((GDN_CHUNK, hd), F32)
    for h in range(GDN_HEADS):
        state = st_ref[h]
        for n in range(ts // GDN_CHUNK):
            r0 = n * GDN_CHUNK
            t0 = (n // 2) * GDN_TILE
            r = _dot(wq_ref[0, h, 2 * r0:2 * r0 + GDN_TILE, :], state.astype(BF16))
            v_new = u_ref[0, h, r0:r0 + GDN_CHUNK, :] - r[0:GDN_CHUNK]
            vpad = jnp.concatenate([v_new, zeros] if n % 2 == 0 else [zeros, v_new], axis=0)
            vpad = vpad.astype(BF16)
            o = r[GDN_CHUNK:] + _dot(attn_ref[0, h, r0:r0 + GDN_CHUNK, :], vpad)
            state = state * eg_ref[0, h, 8 * n:8 * n + 1, :] + _dot(kdt_ref[0, h, t0:t0 + GDN_TILE, :], vpad)
            on = o * lax.rsqrt(jnp.mean(o * o, axis=-1, keepdims=True) + EPS) * gn
            gate = _silu(zg_ref[0, r0:r0 + GDN_CHUNK, h * hd:(h + 1) * hd])
            o_ref[0, r0:r0 + GDN_CHUNK, h * hd:(h + 1) * hd] = (on * gate).astype(BF16)
        st_ref[h] = state


def _gdn_scan(u, wq, kdt, attn, eg, zg, gn, *, ts):
    b, nh, seq, hd = u.shape
    hspec = lambda rows: pl.BlockSpec((1, nh, rows, hd), lambda bi, ti: (bi, 0, ti, 0))
    return pl.pallas_call(
        functools.partial(_gdn_scan_body, ts=ts),
        grid=(b, seq // ts),
        in_specs=[hspec(ts), hspec(2 * ts), hspec(ts), hspec(ts), hspec(ts // 8),
                  pl.BlockSpec((1, ts, nh * hd), lambda bi, ti: (bi, ti, 0)),
                  pl.BlockSpec((1, hd), lambda bi, ti: (0, 0))],
        out_specs=pl.BlockSpec((1, ts, nh * hd), lambda bi, ti: (bi, ti, 0)),
        out_shape=jax.ShapeDtypeStruct((b, seq, nh * hd), BF16),
        scratch_shapes=[pltpu.VMEM((nh, hd, hd), F32)],
        compiler_params=_cparams(("parallel", "arbitrary")),
        name="gdn_scan",
    )(u, wq, kdt, attn, eg, zg, gn)


def _outproj_body(x_ref, a_ref, b_ref, w_ref, o_ref):
    half = a_ref.shape[1]
    o_ref[...] = x_ref[...] + _dot(a_ref[...], w_ref[0:half, :]) + _dot(b_ref[...], w_ref[half:, :])


def _outproj(x, o_nsa, o_gdn, w_out, *, tm):
    t, d = x.shape
    row = lambda i: (i, 0)
    return pl.pallas_call(
        _outproj_body,
        grid=(t // tm,),
        in_specs=[pl.BlockSpec((tm, d), row), pl.BlockSpec((tm, o_nsa.shape[1]), row),
                  pl.BlockSpec((tm, o_gdn.shape[1]), row), pl.BlockSpec(w_out.shape, lambda i: (0, 0))],
        out_specs=pl.BlockSpec((tm, d), row),
        out_shape=jax.ShapeDtypeStruct((t, d), F32),
        compiler_params=_cparams(("parallel",)),
        name="outproj",
    )(x, o_nsa, o_gdn, w_out)


def _ple_body(x_ref, p_ref, nw_ref, wg_ref, wp_ref, fn_ref, o_ref, *, final):
    x = x_ref[...]
    h = _rms(x, nw_ref[...]).astype(BF16)
    gate = _sigmoid(_dot(h, wg_ref[...]))
    out = x + gate * _dot(p_ref[...].astype(BF16), wp_ref[...])
    if final:
        out = _rms(out, fn_ref[...])
    o_ref[...] = out


def _ple(x, p, nw, wg, wp, fn, *, tm, final):
    t, d = x.shape
    row = lambda i: (i, 0)
    const = lambda a: pl.BlockSpec(a.shape, lambda i: (0, 0))
    return pl.pallas_call(
        functools.partial(_ple_body, final=final),
        grid=(t // tm,),
        in_specs=[pl.BlockSpec((tm, d), row), pl.BlockSpec((tm, p.shape[1]), row),
                  const(nw), const(wg), const(wp), const(fn)],
        out_specs=pl.BlockSpec((tm, d), row),
        out_shape=jax.ShapeDtypeStruct((t, d), F32),
        compiler_params=_cparams(("parallel",)),
        name="ple",
    )(x, p, nw, wg, wp, fn)


def _pack_w_in(w_in):
    offs = [0]
    for s in IN_SIZES:
        offs.append(offs[-1] + s)
    main = w_in[:, offs[0]:offs[7]]
    gates = w_in[:, offs[7]:offs[8]]
    gdn = w_in[:, offs[8]:offs[10]]
    ab = w_in[:, offs[10]:offs[12]]
    pad = jnp.zeros((w_in.shape[0], LANES - gates.shape[1] - ab.shape[1]), w_in.dtype)
    return jnp.concatenate([main, gdn, gates, ab, pad], axis=1).astype(BF16)


def _rope_tables(seq):
    dim = NSA_HEAD_DIM
    inv = 1.0 / (ROPE_THETA ** (jnp.arange(0, dim, 2, dtype=F32) / dim))
    ang = jnp.arange(seq, dtype=F32)[:, None] * inv[None, :]
    ang = jnp.concatenate([ang, ang], axis=-1)
    cos, sin = jnp.cos(ang), jnp.sin(ang)
    sign = jnp.where(jnp.arange(dim) < dim // 2, -1.0, 1.0).astype(F32)
    return jnp.tile(cos, (1, LANES // dim)), jnp.tile(sin * sign[None, :], (1, LANES // dim))


def _overlap_t(seq, nc_pad):
    n_slc = seq // SLC_BLOCK
    jc = jnp.arange(nc_pad)[None, :]
    js = jnp.arange(LANES)[:, None]
    ov = ((jc * CMP_STRIDE < (js + 1) * SLC_BLOCK) & (jc * CMP_STRIDE + CMP_BLOCK > js * SLC_BLOCK)
          & (js < n_slc))
    return ov.astype(BF16)


def _layer(x2, p_i, prm, consts, *, b, seq, final, cfg):
    t = b * seq
    hkv, dh = NSA_KV_HEADS, NSA_HEAD_DIM
    x2 = _ffn(x2, prm['ffn1_norm'], prm['ffn1_w1'], prm['ffn1_w3'], prm['ffn1_w2'],
              tm=cfg['ffn_tm'], tf=cfg['ffn_tf'])

    q, kv, cmpf, qkv, zg, small = _inproj(x2, prm['mix_norm'], prm['w_in'], consts['cos'],
                                          consts['sin'], tm=cfg['in_tm'], seq=seq)
    nh = seq // CMP_STRIDE
    cm = cmpf.reshape(b, nh, CMP_STRIDE, 2, hkv, dh).transpose(3, 0, 4, 1, 2, 5)
    cm = cm.reshape(2, b, hkv, nh, CMP_STRIDE * dh)
    kcmp, vcmp = _compress(cm[0], cm[1], prm['cmp_pe_k'], prm['cmp_pe_v'], prm['cmp_k_w1'],
                           prm['cmp_k_w2'], prm['cmp_v_w1'], prm['cmp_v_w2'])
    kvh = kv.reshape(b, seq, 6, hkv, dh).transpose(2, 0, 3, 1, 4)
    ks, vs, kw, vw = kvh[2], kvh[3], kvh[4], kvh[5]
    kaug = jnp.concatenate([consts['onehot'], ks, jnp.zeros_like(ks)], axis=-1)
    padw = ((0, 0), (0, 0), (WINDOW, 0), (0, 0))
    q_pad = jnp.pad(q.reshape(b, seq, NSA_HEADS, dh), ((0, 0), (0, 0), (0, 0), (0, LANES - dh)))
    q_pad = q_pad.reshape(b, seq, NSA_HEADS * LANES)
    nqb = seq // Q_BLOCK
    gates = small[:, 0:SMALL_GATE].reshape(b, nqb, Q_BLOCK, hkv, NSA_GROUP, 3)
    gates = gates.transpose(0, 3, 1, 4, 2, 5).reshape(b, hkv, nqb, NSA_GROUP * Q_BLOCK, 3)
    o = _nsa(q_pad, kcmp, vcmp, kaug, vs, jnp.pad(kw, padw), jnp.pad(vw, padw), gates,
             consts['ovt'], tk=cfg['nsa_tk'])
    o_nsa = o.reshape(b, hkv, nqb, NSA_GROUP, Q_BLOCK, dh).transpose(0, 2, 4, 1, 3, 5)
    o_nsa = o_nsa.reshape(t, NSA_WIDTH)
    u, wq, kdt, attn, eg = _gdn_prep(qkv.reshape(b, seq, 3 * GDN_WIDTH), prm['gdn_conv'],
                                     small.reshape(b, seq, LANES), prm['gdn_a_log'],
                                     prm['gdn_dt_bias'], ts=cfg['prep_ts'])
    o_gdn = _gdn_scan(u, wq, kdt, attn, eg, zg.reshape(b, seq, GDN_WIDTH), prm['gdn_norm'],
                      ts=cfg['scan_ts'])
    x2 = _outproj(x2, o_nsa, o_gdn.reshape(t, GDN_WIDTH), prm['w_out'], tm=cfg['out_tm'])

    x2 = _ffn(x2, prm['ffn2_norm'], prm['ffn2_w1'], prm['ffn2_w3'], prm['ffn2_w2'],
              tm=cfg['ffn_tm'], tf=cfg['ffn_tf'])
    return _ple(x2, p_i, prm['ple_norm'], prm['ple_gate'], prm['ple_proj'], consts['final_norm'],
                tm=cfg['ple_tm'], final=final)


DEFAULT_CFG = dict(ffn_tm=512, ffn_tf=1408, in_tm=512, nsa_tk=256, prep_ts=256, scan_ts=256,
                   out_tm=512, ple_tm=512)


def _forward(x, p, w, cfg):
    b, seq, d = x.shape
    depth = p.shape[0]
    t = b * seq
    cos2, sin2 = _rope_tables(seq)
    nc_pad = seq // CMP_STRIDE
    blk_of_key = jnp.arange(seq) // SLC_BLOCK
    onehot = (blk_of_key[:, None] == jnp.arange(LANES)[None, :]).astype(BF16)
    consts = dict(cos=cos2, sin=sin2, ovt=_overlap_t(seq, nc_pad),
                  onehot=jnp.broadcast_to(onehot, (b, NSA_KV_HEADS, seq, LANES)),
                  final_norm=w['final_norm'].reshape(1, d))
    bf = lambda a: a.astype(BF16)
    lane_row = lambda v, off: jnp.zeros((1, LANES), F32).at[0, off:off + v.shape[0]].set(v)
    half = CMP_STRIDE * NSA_HEAD_DIM
    x2 = x.reshape(t, d)
    for i in range(depth):
        w2pad = lambda a: a
        prm = dict(
            ffn1_norm=w['ffn1_norm'][i].reshape(1, d), ffn1_w1=bf(w['ffn1_w1'][i]),
            ffn1_w3=bf(w['ffn1_w3'][i]), ffn1_w2=bf(w['ffn1_w2'][i]),
            mix_norm=w['mix_norm'][i].reshape(1, d), w_in=_pack_w_in(w['w_in'][i]),
            cmp_pe_k=w['cmp_pe_k'][i].reshape(2, half), cmp_pe_v=w['cmp_pe_v'][i].reshape(2, half),
            cmp_k_w1=bf(w['cmp_k_w1'][i]), cmp_k_w2=w2pad(bf(w['cmp_k_w2'][i])),
            cmp_v_w1=bf(w['cmp_v_w1'][i]), cmp_v_w2=w2pad(bf(w['cmp_v_w2'][i])),
            gdn_conv=w['gdn_conv'][i], gdn_a_log=lane_row(w['gdn_a_log'][i], SMALL_A),
            gdn_dt_bias=lane_row(w['gdn_dt_bias'][i], SMALL_A),
            gdn_norm=w['gdn_norm'][i].reshape(1, GDN_HEAD_DIM), w_out=bf(w['w_out'][i]),
            ffn2_norm=w['ffn2_norm'][i].reshape(1, d), ffn2_w1=bf(w['ffn2_w1'][i]),
            ffn2_w3=bf(w['ffn2_w3'][i]), ffn2_w2=bf(w['ffn2_w2'][i]),
            ple_norm=w['ple_norm'][i].reshape(1, d), ple_gate=bf(w['ple_gate'][i]),
            ple_proj=bf(w['ple_proj'][i]),
        )
        x2 = _layer(x2, p[i].reshape(t, p.shape[-1]), prm, consts, b=b, seq=seq,
                    final=(i == depth - 1), cfg=cfg)
    return x2.reshape(b, seq, d)


def kernel(x, p, ffn1_norm, ffn1_w1, ffn1_w3, ffn1_w2, mix_norm, w_in, cmp_pe_k, cmp_pe_v,
           cmp_k_w1, cmp_k_w2, cmp_v_w1, cmp_v_w2, gdn_conv, gdn_a_log, gdn_dt_bias, gdn_norm,
           w_out, ffn2_norm, ffn2_w1, ffn2_w3, ffn2_w2, ple_norm, ple_gate, ple_proj, final_norm):
    w = dict(ffn1_norm=ffn1_norm, ffn1_w1=ffn1_w1, ffn1_w3=ffn1_w3, ffn1_w2=ffn1_w2,
             mix_norm=mix_norm, w_in=w_in, cmp_pe_k=cmp_pe_k, cmp_pe_v=cmp_pe_v,
             cmp_k_w1=cmp_k_w1, cmp_k_w2=cmp_k_w2, cmp_v_w1=cmp_v_w1, cmp_v_w2=cmp_v_w2,
             gdn_conv=gdn_conv, gdn_a_log=gdn_a_log, gdn_dt_bias=gdn_dt_bias, gdn_norm=gdn_norm,
             w_out=w_out, ffn2_norm=ffn2_norm, ffn2_w1=ffn2_w1, ffn2_w3=ffn2_w3, ffn2_w2=ffn2_w2,
             ple_norm=ple_norm, ple_gate=ple_gate, ple_proj=ple_proj, final_norm=final_norm)
    return _forward(x, p, w, DEFAULT_CFG)
```

```python
import functools

import jax
import jax.numpy as jnp
from jax import lax
from jax.experimental import pallas as pl
from jax.experimental.pallas import tpu as pltpu

F32 = jnp.float32
BF16 = jnp.bfloat16

D_MODEL = 1024
NSA_HEADS = 8
NSA_KV_HEADS = 2
NSA_GROUP = NSA_HEADS // NSA_KV_HEADS
NSA_HEAD_DIM = 64
CMP_BLOCK = 32
CMP_STRIDE = 16
CMP_HIDDEN = 128
SLC_BLOCK = 64
N_SELECTED = 16
WINDOW = 512
Q_BLOCK = 128
GDN_HEADS = 4
GDN_HEAD_DIM = 128
GDN_CHUNK = 64
CONV_WIDTH = 4
D_FF = 2816
PLE_DIM = 256
ROPE_THETA = 10000.0
EPS = 1e-6
FORCE_SCORE = 1e6
NEG_INF = -1e30

NSA_WIDTH = NSA_HEADS * NSA_HEAD_DIM
NSA_KV_WIDTH = NSA_KV_HEADS * NSA_HEAD_DIM
GDN_WIDTH = GDN_HEADS * GDN_HEAD_DIM
IN_SIZES = (NSA_WIDTH, NSA_KV_WIDTH, NSA_KV_WIDTH, NSA_KV_WIDTH, NSA_KV_WIDTH,
            NSA_KV_WIDTH, NSA_KV_WIDTH, 3 * NSA_HEADS, 3 * GDN_WIDTH, GDN_WIDTH,
            GDN_HEADS, GDN_HEADS)

LANES = 128
NSA_MAIN = NSA_WIDTH + 6 * NSA_KV_WIDTH
GDN_MAIN = 4 * GDN_WIDTH
SMALL_GATE = 3 * NSA_HEADS
SMALL_A = SMALL_GATE
SMALL_B = SMALL_GATE + GDN_HEADS
W_IN_PACKED = NSA_MAIN + GDN_MAIN + LANES
SEL_BIAS = -2.0 ** 100
M_INIT = -3.0e38
VMEM_LIMIT = 56 * 1024 * 1024


def _cparams(sem):
    return pltpu.CompilerParams(dimension_semantics=sem, vmem_limit_bytes=VMEM_LIMIT)


def _rms(x, w):
    ms = jnp.mean(x * x, axis=-1, keepdims=True)
    return x * lax.rsqrt(ms + EPS) * w


def _sigmoid(x):
    return 1.0 / (1.0 + jnp.exp(-x))


def _silu(x):
    return x * _sigmoid(x)


def _dot(a, b):
    return jnp.dot(a, b, preferred_element_type=F32)


def _dot_nt(a, b):
    return lax.dot_general(a, b, (((1,), (1,)), ((), ())), preferred_element_type=F32)


def _split3(x):
    hi = x.astype(BF16)
    r = x - hi.astype(F32)
    mid = r.astype(BF16)
    lo = (r - mid.astype(F32)).astype(BF16)
    return hi, mid, lo


def _dot_exact_lhs(a_bf, x):
    hi, mid, lo = _split3(x)
    return _dot(a_bf, hi) + (_dot(a_bf, mid) + _dot(a_bf, lo))


def _dot3(a, b):
    ah, am, al = _split3(a)
    bh, bm, bl = _split3(b)
    small = _dot(ah, bl) + _dot(al, bh) + _dot(am, bm)
    mid = _dot(ah, bm) + _dot(am, bh)
    return _dot(ah, bh) + (mid + small)


def _ffn_body(x_ref, nw_ref, w1_ref, w3_ref, w2_ref, o_ref, h_ref, acc_ref):
    j = pl.program_id(1)

    @pl.when(j == 0)
    def _():
        h_ref[...] = _rms(x_ref[...], nw_ref[...]).astype(BF16)
        acc_ref[...] = jnp.zeros_like(acc_ref)

    h = h_ref[...]
    u = _dot(h, w1_ref[...])
    g = _dot(h, w3_ref[...])
    a = (_silu(u) * g).astype(BF16)
    acc_ref[...] += _dot(a, w2_ref[...])

    @pl.when(j == pl.num_programs(1) - 1)
    def _():
        o_ref[...] = x_ref[...] + 0.5 * acc_ref[...]


def _ffn(x, nw, w1, w3, w2, *, tm, tf):
    t, d = x.shape
    ff = w1.shape[1]
    return pl.pallas_call(
        _ffn_body,
        grid=(t // tm, ff // tf),
        in_specs=[
            pl.BlockSpec((tm, d), lambda i, j: (i, 0)),
            pl.BlockSpec((1, d), lambda i, j: (0, 0)),
            pl.BlockSpec((d, tf), lambda i, j: (0, j)),
            pl.BlockSpec((d, tf), lambda i, j: (0, j)),
            pl.BlockSpec((tf, d), lambda i, j: (j, 0)),
        ],
        out_specs=pl.BlockSpec((tm, d), lambda i, j: (i, 0)),
        out_shape=jax.ShapeDtypeStruct((t, d), F32),
        scratch_shapes=[pltpu.VMEM((tm, d), BF16), pltpu.VMEM((tm, d), F32)],
        compiler_params=_cparams(("parallel", "arbitrary")),
        name="ffn",
    )(x, nw, w1, w3, w2)


def _rope(xg, cos, sin_signed, first_half):
    fwd = pltpu.roll(xg, LANES - NSA_HEAD_DIM // 2, 1)
    bwd = pltpu.roll(xg, NSA_HEAD_DIM // 2, 1)
    return xg * cos + jnp.where(first_half, fwd, bwd) * sin_signed


def _inproj_body(x_ref, nw_ref, w_ref, cos_ref, sin_ref,
                 q_ref, kv_ref, cmpf_ref, qkv_ref, zg_ref, small_ref):
    h = _rms(x_ref[...], nw_ref[...]).astype(BF16)
    cos = cos_ref[...]
    sin_s = sin_ref[...]
    lane = lax.broadcasted_iota(jnp.int32, (1, LANES), 1)
    first_half = (lane & (NSA_HEAD_DIM - 1)) < (NSA_HEAD_DIM // 2)

    z = _dot(h, w_ref[:, 0:NSA_MAIN])
    scale = NSA_HEAD_DIM ** -0.5
    for c in range(NSA_WIDTH // LANES):
        zq = _rope(z[:, c * LANES:(c + 1) * LANES], cos, sin_s, first_half)
        q_ref[:, c * LANES:(c + 1) * LANES] = (zq * scale).astype(BF16)
    base = NSA_WIDTH
    for c in range(6):
        zc = z[:, base + c * LANES: base + (c + 1) * LANES]
        if c % 2 == 0:
            zc = _rope(zc, cos, sin_s, first_half)
        kv_ref[:, c * LANES:(c + 1) * LANES] = zc.astype(BF16)
        if c < 2:
            cmpf_ref[:, c * LANES:(c + 1) * LANES] = zc

    zg = _dot(h, w_ref[:, NSA_MAIN:NSA_MAIN + GDN_MAIN])
    qkv_ref[...] = zg[:, 0:3 * GDN_WIDTH]
    zg_ref[...] = zg[:, 3 * GDN_WIDTH:]

    zs = _dot(h, w_ref[:, NSA_MAIN + GDN_MAIN:])
    is_raw = (lane >= SMALL_A) & (lane < SMALL_B)
    small_ref[...] = jnp.where(is_raw, zs, _sigmoid(zs))


def _inproj(x, nw, w_packed, cos2, sin2, *, tm, seq):
    t, d = x.shape
    nseq = seq // tm
    row = lambda i: (i, 0)
    return pl.pallas_call(
        _inproj_body,
        grid=(t // tm,),
        in_specs=[
            pl.BlockSpec((tm, d), row),
            pl.BlockSpec((1, d), lambda i: (0, 0)),
            pl.BlockSpec((d, W_IN_PACKED), lambda i: (0, 0)),
            pl.BlockSpec((tm, LANES), lambda i: (i % nseq, 0)),
            pl.BlockSpec((tm, LANES), lambda i: (i % nseq, 0)),
        ],
        out_specs=[
            pl.BlockSpec((tm, NSA_WIDTH), row),
            pl.BlockSpec((tm, 6 * NSA_KV_WIDTH), row),
            pl.BlockSpec((tm, 2 * NSA_KV_WIDTH), row),
            pl.BlockSpec((tm, 3 * GDN_WIDTH), row),
            pl.BlockSpec((tm, GDN_WIDTH), row),
            pl.BlockSpec((tm, LANES), row),
        ],
        out_shape=[
            jax.ShapeDtypeStruct((t, NSA_WIDTH), BF16),
            jax.ShapeDtypeStruct((t, 6 * NSA_KV_WIDTH), BF16),
            jax.ShapeDtypeStruct((t, 2 * NSA_KV_WIDTH), F32),
            jax.ShapeDtypeStruct((t, 3 * GDN_WIDTH), F32),
            jax.ShapeDtypeStruct((t, GDN_WIDTH), F32),
            jax.ShapeDtypeStruct((t, LANES), F32),
        ],
        compiler_params=_cparams(("parallel",)),
        name="inproj",
    )(x, nw, w_packed, cos2, sin2)


def _compress_body(xk_ref, xv_ref, pek_ref, pev_ref, w1k_ref, w2k_ref, w1v_ref, w2v_ref,
                   ok_ref, ov_ref):
    half = CMP_STRIDE * NSA_HEAD_DIM

    def one(x_ref, pe_ref, w1_ref, w2_ref):
        x = x_ref[0, 0]
        nh = x.shape[0]
        a = _dot((x + pe_ref[0:1, :]).astype(BF16), w1_ref[0:half, :])
        b = _dot((x + pe_ref[1:2, :]).astype(BF16), w1_ref[half:2 * half, :])
        hid = a + pltpu.roll(b, nh - 1, 0)
        return _dot(_silu(hid).astype(BF16), w2_ref[...])

    ok_ref[0, 0] = one(xk_ref, pek_ref, w1k_ref, w2k_ref).astype(BF16)
    vt = jnp.transpose(one(xv_ref, pev_ref, w1v_ref, w2v_ref))
    ov_ref[0, 0] = vt[0:NSA_HEAD_DIM].astype(BF16)


def _compress(xk, xv, pek, pev, w1k, w2k, w1v, w2v):
    b, hkv, nh, wide = xk.shape
    xspec = pl.BlockSpec((1, 1, nh, wide), lambda i, j: (i, j, 0, 0))
    full = lambda a: pl.BlockSpec(a.shape, lambda i, j: (0,) * a.ndim)
    return pl.pallas_call(
        _compress_body,
        grid=(b, hkv),
        in_specs=[xspec, xspec, full(pek), full(pev), full(w1k), full(w2k), full(w1v), full(w2v)],
        out_specs=[pl.BlockSpec((1, 1, nh, LANES), lambda i, j: (i, j, 0, 0)),
                   pl.BlockSpec((1, 1, NSA_HEAD_DIM, nh), lambda i, j: (i, j, 0, 0))],
        out_shape=[jax.ShapeDtypeStruct((b, hkv, nh, LANES), BF16),
                   jax.ShapeDtypeStruct((b, hkv, NSA_HEAD_DIM, nh), BF16)],
        compiler_params=_cparams(("parallel", "parallel")),
        name="compress",
    )(xk, xv, pek, pev, w1k, w2k, w1v, w2v)


def _nsa_body(q_ref, kc_ref, vc_ref, kaug_ref, vs_ref, kw_ref, vw_ref, g_ref, ovt_ref,
              o_ref, *, n_slc, n_sel, tk):
    i = pl.program_id(2)
    t0 = i * Q_BLOCK
    cols = NSA_GROUP * Q_BLOCK
    dh = NSA_HEAD_DIM
    q_t = q_ref[0, 0, 0]
    lane = lax.broadcasted_iota(jnp.int32, (1, cols), 1)
    tq = t0 + (lane & (Q_BLOCK - 1))

    def softmax_cols(s, mask):
        s = jnp.where(mask, s, NEG_INF)
        m = jnp.max(s, axis=0, keepdims=True)
        e = jnp.where(mask, jnp.exp(s - m), 0.0)
        l = jnp.sum(e, axis=0, keepdims=True)
        return e, 1.0 / jnp.where(l > 0.0, l, 1.0)

    kc = kc_ref[0, 0]
    nc = kc.shape[0]
    crow = lax.broadcasted_iota(jnp.int32, (nc, 1), 0)
    e_c, inv_c = softmax_cols(_dot(kc, q_t), (crow * CMP_STRIDE + (CMP_BLOCK - 1)) <= tq)
    o_cmp = _dot(vc_ref[0, 0], e_c.astype(BF16)) * inv_c

    p_c = e_c * inv_c
    pg = p_c[:, 0:Q_BLOCK]
    for g in range(1, NSA_GROUP):
        pg = pg + p_c[:, g * Q_BLOCK:(g + 1) * Q_BLOCK]
    imp = _dot_exact_lhs(ovt_ref[...], pg)

    blk = lax.broadcasted_iota(jnp.int32, (LANES, Q_BLOCK), 0)
    tcol = t0 + lax.broadcasted_iota(jnp.int32, (LANES, Q_BLOCK), 1)
    cur = jnp.right_shift(tcol, SLC_BLOCK.bit_length() - 1)
    forced = (blk == 0) | (blk == cur) | (blk == cur - 1)
    valid = blk * SLC_BLOCK <= tcol
    score = jnp.where(forced, FORCE_SCORE, jnp.where(valid, imp, -1.0))
    score = jnp.where(blk < n_slc, score, -jnp.inf)
    bias = jnp.full((LANES, Q_BLOCK), SEL_BIAS, F32)
    for _ in range(n_sel):
        mx = jnp.max(score, axis=0, keepdims=True)
        idx = jnp.min(jnp.where(score == mx, blk, 2 * LANES), axis=0, keepdims=True)
        hit = blk == idx
        bias = jnp.where(hit, 0.0, bias)
        score = jnp.where(hit, -jnp.inf, score)
    bias = bias.astype(BF16)
    q_aug = jnp.concatenate([jnp.concatenate([bias] * NSA_GROUP, axis=1), q_t], axis=0)

    def slc_step(j, carry, causal):
        m, l, acc = carry
        k0 = pl.multiple_of(j * tk, tk)
        s = _dot(kaug_ref[0, 0, pl.ds(k0, tk), :], q_aug)
        if causal:
            kpos = k0 + lax.broadcasted_iota(jnp.int32, (tk, 1), 0)
            s = jnp.where(kpos <= tq, s, NEG_INF)
        m_new = jnp.maximum(m, jnp.max(s, axis=0, keepdims=True))
        alpha = jnp.exp(m - m_new)
        p = jnp.exp(s - m_new)
        l = alpha * l + jnp.sum(p, axis=0, keepdims=True)
        acc = alpha * acc + _dot(vs_ref[0, 0, j], p.astype(BF16))
        return m_new, l, acc

    n_full = t0 // tk
    init = (jnp.full((1, cols), M_INIT, F32), jnp.zeros((1, cols), F32), jnp.zeros((dh, cols), F32))
    carry = lax.fori_loop(0, n_full, functools.partial(slc_step, causal=False), init)
    _, l_s, acc_s = slc_step(n_full, carry, True)
    o_slc = acc_s * (1.0 / l_s)

    wspan = Q_BLOCK + WINDOW
    c0 = jnp.maximum(i - WINDOW // Q_BLOCK, 0)
    start = pl.multiple_of(c0 * Q_BLOCK, Q_BLOCK)
    kpos = start + lax.broadcasted_iota(jnp.int32, (wspan, 1), 0)
    dist = tq - kpos
    e_w, inv_w = softmax_cols(_dot(kw_ref[0, 0, pl.ds(start, wspan), :], q_t),
                              (dist >= 0) & (dist < WINDOW))
    e_w = e_w.astype(BF16)
    o_win = _dot(vw_ref[0, 0, c0], e_w[0:Q_BLOCK])
    for c in range(1, wspan // Q_BLOCK):
        o_win = o_win + _dot(vw_ref[0, 0, c0 + c], e_w[c * Q_BLOCK:(c + 1) * Q_BLOCK])
    o_win = o_win * inv_w

    g = g_ref[0, 0, 0]
    out = g[0:1, :] * o_cmp + g[1:2, :] * o_slc + g[2:3, :] * o_win
    o_ref[0, 0, 0] = out.astype(BF16)


def _nsa(q_t, kcmp, vcmp_t, kaug, vs_t, kw, vw_t, gates_t, ovt, *, tk):
    b, hkv, nqb, _, cols = q_t.shape
    seq = kaug.shape[2]
    n_slc = seq // SLC_BLOCK
    per_head = lambda a: pl.BlockSpec((1, 1) + a.shape[2:],
                                      lambda bi, hi, qi: (bi, hi) + (0,) * (a.ndim - 2))
    per_blk = lambda a: pl.BlockSpec((1, 1, 1) + a.shape[3:], lambda bi, hi, qi: (bi, hi, qi, 0, 0))
    body = functools.partial(_nsa_body, n_slc=n_slc, n_sel=min(N_SELECTED, n_slc), tk=tk)
    return pl.pallas_call(
        body,
        grid=(b, hkv, nqb),
        in_specs=[per_blk(q_t), per_head(kcmp), per_head(vcmp_t), per_head(kaug), per_head(vs_t),
                  per_head(kw), per_head(vw_t), per_blk(gates_t),
                  pl.BlockSpec(ovt.shape, lambda bi, hi, qi: (0, 0))],
        out_specs=pl.BlockSpec((1, 1, 1, NSA_HEAD_DIM, cols), lambda bi, hi, qi: (bi, hi, qi, 0, 0)),
        out_shape=jax.ShapeDtypeStruct((b, hkv, nqb, NSA_HEAD_DIM, cols), BF16),
        compiler_params=_cparams(("parallel", "parallel", "arbitrary")),
        name="nsa",
    )(q_t, kcmp, vcmp_t, kaug, vs_t, kw, vw_t, gates_t, ovt)


GDN_TILE = 2 * GDN_CHUNK


def _gdn_prep_body(qkv_ref, halo_ref, cw_ref, small_ref, alog_ref, dtb_ref,
                   u_ref, wq_ref, kdt_ref, attn_ref, eg_ref, *, ts):
    i = pl.program_id(1)
    x = qkv_ref[0]
    halo = jnp.where(i > 0, halo_ref[0], 0.0)
    xx = jnp.concatenate([halo, x], axis=0)
    y = x * cw_ref[CONV_WIDTH - 1:CONV_WIDTH, :]
    for d in range(1, CONV_WIDTH):
        shifted = pltpu.roll(xx, d, 0)[8:]
        y = y + shifted * cw_ref[CONV_WIDTH - 1 - d:CONV_WIDTH - d, :]
    y = _silu(y)

    sm = small_ref[0]
    sp_in = sm + dtb_ref[...]
    softplus = jnp.maximum(sp_in, 0.0) + jnp.log(1.0 + jnp.exp(-jnp.abs(sp_in)))
    glog = -jnp.exp(alog_ref[...]) * softplus

    ri = lax.broadcasted_iota(jnp.int32, (ts, ts), 0)
    ci = lax.broadcasted_iota(jnp.int32, (ts, ts), 1)
    sh = GDN_CHUNK.bit_length() - 1
    same = jnp.right_shift(ri, sh) == jnp.right_shift(ci, sh)
    tril = jnp.where(same & (ri >= ci), 1.0, 0.0).astype(BF16)
    ones = jnp.where(same, 1.0, 0.0).astype(BF16)
    gcum = _dot_exact_lhs(tril, glog)
    glast = _dot_exact_lhs(ones, glog)

    r2 = lax.broadcasted_iota(jnp.int32, (GDN_TILE, GDN_TILE), 0)
    c2 = lax.broadcasted_iota(jnp.int32, (GDN_TILE, GDN_TILE), 1)
    same2 = jnp.right_shift(r2, sh) == jnp.right_shift(c2, sh)
    incl = same2 & (r2 >= c2)
    strict = same2 & (r2 > c2)
    eye = jnp.where(r2 == c2, 1.0, 0.0)
    qscale = GDN_HEAD_DIM ** -0.5

    for c in range(ts // GDN_TILE):
        r0 = c * GDN_TILE
        gc_tile = gcum[r0:r0 + GDN_TILE]
        gc_rows = jnp.transpose(gc_tile)
        for h in range(GDN_HEADS):
            lo = h * GDN_HEAD_DIM
            qh = y[r0:r0 + GDN_TILE, lo:lo + GDN_HEAD_DIM]
            kh = y[r0:r0 + GDN_TILE, GDN_WIDTH + lo:GDN_WIDTH + lo + GDN_HEAD_DIM]
            vh = y[r0:r0 + GDN_TILE, 2 * GDN_WIDTH + lo:2 * GDN_WIDTH + lo + GDN_HEAD_DIM]
            qh = qh * lax.rsqrt(jnp.sum(qh * qh, axis=-1, keepdims=True) + EPS)
            kh = kh * lax.rsqrt(jnp.sum(kh * kh, axis=-1, keepdims=True) + EPS)
            gc_col = gc_tile[:, SMALL_A + h:SMALL_A + h + 1]
            gc_row = gc_rows[SMALL_A + h:SMALL_A + h + 1, :]
            gl_col = glast[r0:r0 + GDN_TILE, SMALL_A + h:SMALL_A + h + 1]
            beta = sm[r0:r0 + GDN_TILE, SMALL_B + h:SMALL_B + h + 1]

            decay = jnp.where(incl, jnp.exp(jnp.minimum(gc_col - gc_row, 0.0)), 0.0)
            kb = kh * beta
            k_bf = kh.astype(BF16)
            a_s = jnp.where(strict, _dot_nt(kb.astype(BF16), k_bf) * decay, 0.0)
            xinv = eye - a_s
            pw = _dot3(a_s, a_s)
            steps = GDN_CHUNK.bit_length() - 2
            for s in range(steps):
                xinv = xinv + _dot3(xinv, pw)
                if s + 1 < steps:
                    pw = _dot3(pw, pw)
            egc = jnp.exp(gc_col)
            u = _dot3(xinv, vh * beta)
            w = _dot3(xinv, kb * egc)
            qs = qh * qscale
            attn = jnp.where(incl, _dot_nt(qs.astype(BF16), k_bf) * decay, 0.0)
            q_dec = qs * egc
            k_dec = kh * jnp.exp(gl_col - gc_col)

            u_ref[0, h, r0:r0 + GDN_TILE, :] = u
            attn_ref[0, h, r0:r0 + GDN_TILE, :] = attn.astype(BF16)
            kdt_ref[0, h, r0:r0 + GDN_TILE, :] = jnp.transpose(k_dec).astype(BF16)
            for cc in range(2):
                a0 = cc * GDN_CHUNK
                wq = jnp.concatenate([w[a0:a0 + GDN_CHUNK], q_dec[a0:a0 + GDN_CHUNK]], axis=0)
                n0 = 2 * r0 + cc * GDN_TILE
                wq_ref[0, h, n0:n0 + GDN_TILE, :] = wq.astype(BF16)
                e0 = (r0 // GDN_CHUNK + cc) * 8
                eg_ref[0, h, e0:e0 + 8, :] = jnp.broadcast_to(
                    jnp.exp(gl_col[a0:a0 + 8]), (8, GDN_HEAD_DIM))


def _gdn_prep(qkv, cw, small, alog_row, dtb_row, *, ts):
    b, seq, wide = qkv.shape
    nt = seq // ts
    hd = GDN_HEAD_DIM
    hspec = lambda rows: pl.BlockSpec((1, GDN_HEADS, rows, hd), lambda bi, ti: (bi, 0, ti, 0))
    hshape = lambda rows, dt: jax.ShapeDtypeStruct((b, GDN_HEADS, rows, hd), dt)
    return pl.pallas_call(
        functools.partial(_gdn_prep_body, ts=ts),
        grid=(b, nt),
        in_specs=[
            pl.BlockSpec((1, ts, wide), lambda bi, ti: (bi, ti, 0)),
            pl.BlockSpec((1, 8, wide), lambda bi, ti: (bi, jnp.maximum(ti * (ts // 8) - 1, 0), 0)),
            pl.BlockSpec(cw.shape, lambda bi, ti: (0, 0)),
            pl.BlockSpec((1, ts, LANES), lambda bi, ti: (bi, ti, 0)),
            pl.BlockSpec((1, LANES), lambda bi, ti: (0, 0)),
            pl.BlockSpec((1, LANES), lambda bi, ti: (0, 0)),
        ],
        out_specs=[hspec(ts), hspec(2 * ts), hspec(ts), hspec(ts), hspec(ts // 8)],
        out_shape=[hshape(seq, F32), hshape(2 * seq, BF16), hshape(seq, BF16),
                   hshape(seq, BF16), hshape(seq // 8, F32)],
        compiler_params=_cparams(("parallel", "parallel")),
        name="gdn_prep",
    )(qkv, qkv, cw, small, alog_row, dtb_row)


def _gdn_scan_body(u_ref, wq_ref, kdt_ref, attn_ref, eg_ref, zg_ref, gn_ref, o_ref, st_ref, *, ts):
    @pl.when(pl.program_id(1) == 0)
    def _():
        st_ref[...] = jnp.zeros_like(st_ref)

    hd = GDN_HEAD_DIM
    gn = gn_ref[...]
    zeros = jnp.zeros((GDN_CHUNK, hd), F32)
    for h in range(GDN_HEADS):
        state = st_ref[h]
        for n in range(ts // GDN_CHUNK):
            r0 = n * GDN_CHUNK
            t0 = (n // 2) * GDN_TILE
            r = _dot(wq_ref[0, h, 2 * r0:2 * r0 + GDN_TILE, :], state.astype(BF16))
            v_new = u_ref[0, h, r0:r0 + GDN_CHUNK, :] - r[0:GDN_CHUNK]
            vpad = jnp.concatenate([v_new, zeros] if n % 2 == 0 else [zeros, v_new], axis=0)
            vpad = vpad.astype(BF16)
            o = r[GDN_CHUNK:] + _dot(attn_ref[0, h, r0:r0 + GDN_CHUNK, :], vpad)
            state = state * eg_ref[0, h, 8 * n:8 * n + 1, :] + _dot(kdt_ref[0, h, t0:t0 + GDN_TILE, :], vpad)
            on = o * lax.rsqrt(jnp.mean(o * o, axis=-1, keepdims=True) + EPS) * gn
            gate = _silu(zg_ref[0, r0:r0 + GDN_CHUNK, h * hd:(h + 1) * hd])
            o_ref[0, r0:r0 + GDN_CHUNK, h * hd:(h + 1) * hd] = (on * gate).astype(BF16)
        st_ref[h] = state


def _gdn_scan(u, wq, kdt, attn, eg, zg, gn, *, ts):
    b, nh, seq, hd = u.shape
    hspec = lambda rows: pl.BlockSpec((1, nh, rows, hd), lambda bi, ti: (bi, 0, ti, 0))
    return pl.pallas_call(
        functools.partial(_gdn_scan_body, ts=ts),
        grid=(b, seq // ts),
        in_specs=[hspec(ts), hspec(2 * ts), hspec(ts), hspec(ts), hspec(ts // 8),
                  pl.BlockSpec((1, ts, nh * hd), lambda bi, ti: (bi, ti, 0)),
                  pl.BlockSpec((1, hd), lambda bi, ti: (0, 0))],
        out_specs=pl.BlockSpec((1, ts, nh * hd), lambda bi, ti: (bi, ti, 0)),
        out_shape=jax.ShapeDtypeStruct((b, seq, nh * hd), BF16),
        scratch_shapes=[pltpu.VMEM((nh, hd, hd), F32)],
        compiler_params=_cparams(("parallel", "arbitrary")),
        name="gdn_scan",
    )(u, wq, kdt, attn, eg, zg, gn)


def _outproj_body(x_ref, a_ref, b_ref, w_ref, o_ref):
    half = a_ref.shape[1]
    o_ref[...] = x_ref[...] + _dot(a_ref[...], w_ref[0:half, :]) + _dot(b_ref[...], w_ref[half:, :])


def _outproj(x, o_nsa, o_gdn, w_out, *, tm):
    t, d = x.shape
    row = lambda i: (i, 0)
    return pl.pallas_call(
        _outproj_body,
        grid=(t // tm,),
        in_specs=[pl.BlockSpec((tm, d), row), pl.BlockSpec((tm, o_nsa.shape[1]), row),
                  pl.BlockSpec((tm, o_gdn.shape[1]), row), pl.BlockSpec(w_out.shape, lambda i: (0, 0))],
        out_specs=pl.BlockSpec((tm, d), row),
        out_shape=jax.ShapeDtypeStruct((t, d), F32),
        compiler_params=_cparams(("parallel",)),
        name="outproj",
    )(x, o_nsa, o_gdn, w_out)


def _ple_body(x_ref, p_ref, nw_ref, wg_ref, wp_ref, fn_ref, o_ref, *, final):
    x = x_ref[...]
    h = _rms(x, nw_ref[...]).astype(BF16)
    gate = _sigmoid(_dot(h, wg_ref[...]))
    out = x + gate * _dot(p_ref[...].astype(BF16), wp_ref[...])
    if final:
        out = _rms(out, fn_ref[...])
    o_ref[...] = out


def _ple(x, p, nw, wg, wp, fn, *, tm, final):
    t, d = x.shape
    row = lambda i: (i, 0)
    const = lambda a: pl.BlockSpec(a.shape, lambda i: (0, 0))
    return pl.pallas_call(
        functools.partial(_ple_body, final=final),
        grid=(t // tm,),
        in_specs=[pl.BlockSpec((tm, d), row), pl.BlockSpec((tm, p.shape[1]), row),
                  const(nw), const(wg), const(wp), const(fn)],
        out_specs=pl.BlockSpec((tm, d), row),
        out_shape=jax.ShapeDtypeStruct((t, d), F32),
        compiler_params=_cparams(("parallel",)),
        name="ple",
    )(x, p, nw, wg, wp, fn)


def _pack_w_in(w_in):
    offs = [0]
    for s in IN_SIZES:
        offs.append(offs[-1] + s)
    main = w_in[:, offs[0]:offs[7]]
    gates = w_in[:, offs[7]:offs[8]]
    gdn = w_in[:, offs[8]:offs[10]]
    ab = w_in[:, offs[10]:offs[12]]
    pad = jnp.zeros((w_in.shape[0], LANES - gates.shape[1] - ab.shape[1]), w_in.dtype)
    return jnp.concatenate([main, gdn, gates, ab, pad], axis=1).astype(BF16)


def _rope_tables(seq):
    dim = NSA_HEAD_DIM
    inv = 1.0 / (ROPE_THETA ** (jnp.arange(0, dim, 2, dtype=F32) / dim))
    ang = jnp.arange(seq, dtype=F32)[:, None] * inv[None, :]
    ang = jnp.concatenate([ang, ang], axis=-1)
    cos, sin = jnp.cos(ang), jnp.sin(ang)
    sign = jnp.where(jnp.arange(dim) < dim // 2, -1.0, 1.0).astype(F32)
    return jnp.tile(cos, (1, LANES // dim)), jnp.tile(sin * sign[None, :], (1, LANES // dim))


def _overlap_t(seq, nc_pad):
    n_slc = seq // SLC_BLOCK
    jc = jnp.arange(nc_pad)[None, :]
    js = jnp.arange(LANES)[:, None]
    ov = ((jc * CMP_STRIDE < (js + 1) * SLC_BLOCK) & (jc * CMP_STRIDE + CMP_BLOCK > js * SLC_BLOCK)
          & (js < n_slc))
    return ov.astype(BF16)


def _layer(x2, p_i, prm, consts, *, b, seq, final, cfg):
    t = b * seq
    hkv, dh = NSA_KV_HEADS, NSA_HEAD_DIM
    x2 = _ffn(x2, prm['ffn1_norm'], prm['ffn1_w1'], prm['ffn1_w3'], prm['ffn1_w2'],
              tm=cfg['ffn_tm'], tf=cfg['ffn_tf'])

    q, kv, cmpf, qkv, zg, small = _inproj(x2, prm['mix_norm'], prm['w_in'], consts['cos'],
                                          consts['sin'], tm=cfg['in_tm'], seq=seq)
    nh = seq // CMP_STRIDE
    cm = cmpf.reshape(b, nh, CMP_STRIDE, 2, hkv, dh).transpose(3, 0, 4, 1, 2, 5)
    cm = cm.reshape(2, b, hkv, nh, CMP_STRIDE * dh)
    kcmp, vcmp_t = _compress(cm[0], cm[1], prm['cmp_pe_k'], prm['cmp_pe_v'], prm['cmp_k_w1'],
                             prm['cmp_k_w2'], prm['cmp_v_w1'], prm['cmp_v_w2'])
    tk = cfg['nsa_tk']
    nqb = seq // Q_BLOCK
    cols = NSA_GROUP * Q_BLOCK
    kvh = kv.reshape(b, seq, 6, hkv, dh).transpose(2, 0, 3, 1, 4)
    ks, vs, kw, vw = kvh[2], kvh[3], kvh[4], kvh[5]
    kaug = jnp.concatenate([consts['onehot'], ks, jnp.zeros_like(ks)], axis=-1)
    vs_t = vs.reshape(b, hkv, seq // tk, tk, dh).transpose(0, 1, 2, 4, 3)
    kw_p = jnp.concatenate([kw, jnp.zeros_like(kw)], axis=-1)
    vw_t = vw.reshape(b, hkv, nqb, Q_BLOCK, dh).transpose(0, 1, 2, 4, 3)
    q_t = q.reshape(b, nqb, Q_BLOCK, hkv, NSA_GROUP, dh).transpose(0, 3, 1, 5, 4, 2)
    q_t = jnp.pad(q_t.reshape(b, hkv, nqb, dh, cols), ((0, 0),) * 3 + ((0, LANES - dh), (0, 0)))
    gates_t = small[:, 0:SMALL_GATE].reshape(b, nqb, Q_BLOCK, hkv, NSA_GROUP, 3)
    gates_t = gates_t.transpose(0, 3, 1, 5, 4, 2).reshape(b, hkv, nqb, 3, cols)
    o = _nsa(q_t, kcmp, vcmp_t, kaug, vs_t, kw_p, vw_t, gates_t, consts['ovt'], tk=tk)
    o_nsa = o.reshape(b, hkv, nqb, dh, NSA_GROUP, Q_BLOCK).transpose(0, 2, 5, 1, 4, 3)
    o_nsa = o_nsa.reshape(t, NSA_WIDTH)
    u, wq, kdt, attn, eg = _gdn_prep(qkv.reshape(b, seq, 3 * GDN_WIDTH), prm['gdn_conv'],
                                     small.reshape(b, seq, LANES), prm['gdn_a_log'],
                                     prm['gdn_dt_bias'], ts=cfg['prep_ts'])
    o_gdn = _gdn_scan(u, wq, kdt, attn, eg, zg.reshape(b, seq, GDN_WIDTH), prm['gdn_norm'],
                      ts=cfg['scan_ts'])
    x2 = _outproj(x2, o_nsa, o_gdn.reshape(t, GDN_WIDTH), prm['w_out'], tm=cfg['out_tm'])

    x2 = _ffn(x2, prm['ffn2_norm'], prm['ffn2_w1'], prm['ffn2_w3'], prm['ffn2_w2'],
              tm=cfg['ffn_tm'], tf=cfg['ffn_tf'])
    return _ple(x2, p_i, prm['ple_norm'], prm['ple_gate'], prm['ple_proj'], consts['final_norm'],
                tm=cfg['ple_tm'], final=final)


DEFAULT_CFG = dict(ffn_tm=512, ffn_tf=1408, in_tm=512, nsa_tk=512, prep_ts=256, scan_ts=256,
                   out_tm=512, ple_tm=512)


def _forward(x, p, w, cfg):
    b, seq, d = x.shape
    depth = p.shape[0]
    t = b * seq
    cos2, sin2 = _rope_tables(seq)
    nc_pad = seq // CMP_STRIDE
    blk_of_key = jnp.arange(seq) // SLC_BLOCK
    onehot = (blk_of_key[:, None] == jnp.arange(LANES)[None, :]).astype(BF16)
    consts = dict(cos=cos2, sin=sin2, ovt=_overlap_t(seq, nc_pad),
                  onehot=jnp.broadcast_to(onehot, (b, NSA_KV_HEADS, seq, LANES)),
                  final_norm=w['final_norm'].reshape(1, d))
    bf = lambda a: a.astype(BF16)
    lane_row = lambda v, off: jnp.zeros((1, LANES), F32).at[0, off:off + v.shape[0]].set(v)
    half = CMP_STRIDE * NSA_HEAD_DIM
    x2 = x.reshape(t, d)
    for i in range(depth):
        w2pad = lambda a: jnp.pad(a, ((0, 0), (0, LANES - a.shape[1])))
        prm = dict(
            ffn1_norm=w['ffn1_norm'][i].reshape(1, d), ffn1_w1=bf(w['ffn1_w1'][i]),
            ffn1_w3=bf(w['ffn1_w3'][i]), ffn1_w2=bf(w['ffn1_w2'][i]),
            mix_norm=w['mix_norm'][i].reshape(1, d), w_in=_pack_w_in(w['w_in'][i]),
            cmp_pe_k=w['cmp_pe_k'][i].reshape(2, half), cmp_pe_v=w['cmp_pe_v'][i].reshape(2, half),
            cmp_k_w1=bf(w['cmp_k_w1'][i]), cmp_k_w2=w2pad(bf(w['cmp_k_w2'][i])),
            cmp_v_w1=bf(w['cmp_v_w1'][i]), cmp_v_w2=w2pad(bf(w['cmp_v_w2'][i])),
            gdn_conv=w['gdn_conv'][i], gdn_a_log=lane_row(w['gdn_a_log'][i], SMALL_A),
            gdn_dt_bias=lane_row(w['gdn_dt_bias'][i], SMALL_A),
            gdn_norm=w['gdn_norm'][i].reshape(1, GDN_HEAD_DIM), w_out=bf(w['w_out'][i]),
            ffn2_norm=w['ffn2_norm'][i].reshape(1, d), ffn2_w1=bf(w['ffn2_w1'][i]),
            ffn2_w3=bf(w['ffn2_w3'][i]), ffn2_w2=bf(w['ffn2_w2'][i]),
            ple_norm=w['ple_norm'][i].reshape(1, d), ple_gate=bf(w['ple_gate'][i]),
            ple_proj=bf(w['ple_proj'][i]),
        )
        x2 = _layer(x2, p[i].reshape(t, p.shape[-1]), prm, consts, b=b, seq=seq,
                    final=(i == depth - 1), cfg=cfg)
    return x2.reshape(b, seq, d)


def kernel(x, p, ffn1_norm, ffn1_w1, ffn1_w3, ffn1_w2, mix_norm, w_in, cmp_pe_k, cmp_pe_v,
           cmp_k_w1, cmp_k_w2, cmp_v_w1, cmp_v_w2, gdn_conv, gdn_a_log, gdn_dt_bias, gdn_norm,
           w_out, ffn2_norm, ffn2_w1, ffn2_w3, ffn2_w2, ple_norm, ple_gate, ple_proj, final_norm):
    w = dict(ffn1_norm=ffn1_norm, ffn1_w1=ffn1_w1, ffn1_w3=ffn1_w3, ffn1_w2=ffn1_w2,
             mix_norm=mix_norm, w_in=w_in, cmp_pe_k=cmp_pe_k, cmp_pe_v=cmp_pe_v,
             cmp_k_w1=cmp_k_w1, cmp_k_w2=cmp_k_w2, cmp_v_w1=cmp_v_w1, cmp_v_w2=cmp_v_w2,
             gdn_conv=gdn_conv, gdn_a_log=gdn_a_log, gdn_dt_bias=gdn_dt_bias, gdn_norm=gdn_norm,
             w_out=w_out, ffn2_norm=ffn2_norm, ffn2_w1=ffn2_w1, ffn2_w3=ffn2_w3, ffn2_w2=ffn2_w2,
             ple_norm=ple_norm, ple_gate=ple_gate, ple_proj=ple_proj, final_norm=final_norm)
    return _forward(x, p, w, DEFAULT_CFG)
```

```python
import functools

import jax
import jax.numpy as jnp
from jax import lax
from jax.experimental import pallas as pl
from jax.experimental.pallas import tpu as pltpu

F32 = jnp.float32
BF16 = jnp.bfloat16

D_MODEL = 1024
NSA_HEADS = 8
NSA_KV_HEADS = 2
NSA_GROUP = NSA_HEADS // NSA_KV_HEADS
NSA_HEAD_DIM = 64
CMP_BLOCK = 32
CMP_STRIDE = 16
CMP_HIDDEN = 128
SLC_BLOCK = 64
N_SELECTED = 16
WINDOW = 512
Q_BLOCK = 128
GDN_HEADS = 4
GDN_HEAD_DIM = 128
GDN_CHUNK = 64
CONV_WIDTH = 4
D_FF = 2816
PLE_DIM = 256
ROPE_THETA = 10000.0
EPS = 1e-6
FORCE_SCORE = 1e6
NEG_INF = -1e30

NSA_WIDTH = NSA_HEADS * NSA_HEAD_DIM
NSA_KV_WIDTH = NSA_KV_HEADS * NSA_HEAD_DIM
GDN_WIDTH = GDN_HEADS * GDN_HEAD_DIM
IN_SIZES = (NSA_WIDTH, NSA_KV_WIDTH, NSA_KV_WIDTH, NSA_KV_WIDTH, NSA_KV_WIDTH,
            NSA_KV_WIDTH, NSA_KV_WIDTH, 3 * NSA_HEADS, 3 * GDN_WIDTH, GDN_WIDTH,
            GDN_HEADS, GDN_HEADS)

LANES = 128
NSA_MAIN = NSA_WIDTH + 6 * NSA_KV_WIDTH
GDN_MAIN = 4 * GDN_WIDTH
GATE_ROWS = 16
SMALL_GATE = NSA_KV_HEADS * GATE_ROWS
SMALL_A = SMALL_GATE
SMALL_B = SMALL_GATE + GDN_HEADS
W_IN_PACKED = NSA_MAIN + GDN_MAIN + LANES
V_ROWS = NSA_HEAD_DIM + 16
SEL_BIAS = -2.0 ** 100
M_INIT = -3.0e38
LOG2_E = 1.4426950408889634
VMEM_LIMIT = 56 * 1024 * 1024


def _cparams(sem):
    return pltpu.CompilerParams(dimension_semantics=sem, vmem_limit_bytes=VMEM_LIMIT)


def _rms(x, w):
    ms = jnp.mean(x * x, axis=-1, keepdims=True)
    return x * lax.rsqrt(ms + EPS) * w


def _sigmoid(x):
    return 1.0 / (1.0 + jnp.exp(-x))


def _silu(x):
    return x * _sigmoid(x)


def _dot(a, b):
    return jnp.dot(a, b, preferred_element_type=F32)


def _dot_nt(a, b):
    return lax.dot_general(a, b, (((1,), (1,)), ((), ())), preferred_element_type=F32)


def _split3(x):
    hi = x.astype(BF16)
    r = x - hi.astype(F32)
    mid = r.astype(BF16)
    lo = (r - mid.astype(F32)).astype(BF16)
    return hi, mid, lo


def _dot_exact_lhs(a_bf, x):
    hi, mid, lo = _split3(x)
    return _dot(a_bf, hi) + (_dot(a_bf, mid) + _dot(a_bf, lo))


def _split2(x):
    hi = x.astype(BF16)
    return hi, (x - hi.astype(F32)).astype(BF16)


def _dot_split(a, b):
    ah, am = a
    bh, bm = b
    return _dot(ah, bh) + (_dot(ah, bm) + _dot(am, bh))


def _layer_spec(block, layer, index):
    return pl.BlockSpec((None,) + block, lambda *g: (layer,) + index(*g))


def _ffn_body(x_ref, nw_ref, w1_ref, w3_ref, w2_ref, o_ref, h_ref, acc_ref):
    j = pl.program_id(1)

    @pl.when(j == 0)
    def _():
        h_ref[...] = _rms(x_ref[...], nw_ref[...]).astype(BF16)
        acc_ref[...] = jnp.zeros_like(acc_ref)

    h = h_ref[...]
    u = _dot(h, w1_ref[...])
    g = _dot(h, w3_ref[...])
    a = (_silu(u) * g).astype(BF16)
    acc_ref[...] += _dot(a, w2_ref[...])

    @pl.when(j == pl.num_programs(1) - 1)
    def _():
        o_ref[...] = x_ref[...] + 0.5 * acc_ref[...]


def _ffn(x, nw, w1, w3, w2, layer, *, tm, tf):
    t, d = x.shape
    ff = w1.shape[2]
    return pl.pallas_call(
        _ffn_body,
        grid=(t // tm, ff // tf),
        in_specs=[
            pl.BlockSpec((tm, d), lambda i, j: (i, 0)),
            _layer_spec((1, d), layer, lambda i, j: (0, 0)),
            _layer_spec((d, tf), layer, lambda i, j: (0, j)),
            _layer_spec((d, tf), layer, lambda i, j: (0, j)),
            _layer_spec((tf, d), layer, lambda i, j: (j, 0)),
        ],
        out_specs=pl.BlockSpec((tm, d), lambda i, j: (i, 0)),
        out_shape=jax.ShapeDtypeStruct((t, d), F32),
        scratch_shapes=[pltpu.VMEM((tm, d), BF16), pltpu.VMEM((tm, d), F32)],
        compiler_params=_cparams(("parallel", "arbitrary")),
        name="ffn",
    )(x, nw, w1, w3, w2)


def _rope(xg, cos, sin_signed, first_half):
    fwd = pltpu.roll(xg, LANES - NSA_HEAD_DIM // 2, 1)
    bwd = pltpu.roll(xg, NSA_HEAD_DIM // 2, 1)
    return xg * cos + jnp.where(first_half, fwd, bwd) * sin_signed


def _inproj_body(x_ref, nw_ref, w_ref, cos_ref, sin_ref,
                 qt_ref, kaug_ref, vst_ref, kw_ref, vwt_ref, gt_ref,
                 cmpk_ref, cmpv_ref, qkv_ref, zg_ref, small_ref, *, nseq):
    tm = x_ref.shape[0]
    nq = tm // Q_BLOCK
    dh = NSA_HEAD_DIM
    h = _rms(x_ref[...], nw_ref[...]).astype(BF16)
    cos = cos_ref[...]
    sin_s = sin_ref[...]
    lane = lax.broadcasted_iota(jnp.int32, (1, LANES), 1)
    first_half = (lane & (dh - 1)) < (dh // 2)
    low = lane < dh

    z = _dot(h, w_ref[:, 0:NSA_MAIN])
    scale = dh ** -0.5 * LOG2_E
    for pair in range(NSA_WIDTH // LANES):
        zq = _rope(z[:, pair * LANES:(pair + 1) * LANES], cos, sin_s, first_half) * scale
        tr = jnp.transpose(zq).astype(BF16)
        for half in range(2):
            hk, g = divmod(2 * pair + half, NSA_GROUP)
            for qb in range(nq):
                qt_ref[0, hk, qb, 0:dh, g * Q_BLOCK:(g + 1) * Q_BLOCK] = (
                    tr[half * dh:(half + 1) * dh, qb * Q_BLOCK:(qb + 1) * Q_BLOCK])
    qt_ref[0, :, :, dh:, :] = jnp.zeros((NSA_KV_HEADS, nq, LANES - dh, NSA_GROUP * Q_BLOCK), BF16)

    def group(c, rotary):
        zc = z[:, NSA_WIDTH + c * LANES: NSA_WIDTH + (c + 1) * LANES]
        return _rope(zc, cos, sin_s, first_half) if rotary else zc

    def heads(zc):
        return [jnp.where(low, zc, 0.0), jnp.where(low, pltpu.roll(zc, dh, 1), 0.0)]

    cmpk_ref[...] = group(0, True)
    cmpv_ref[...] = group(1, False)
    tok = (pl.program_id(0) % nseq) * tm + lax.broadcasted_iota(jnp.int32, (tm, 1), 0)
    onehot = jnp.where(jnp.right_shift(tok, SLC_BLOCK.bit_length() - 1) == lane, 1.0, 0.0).astype(BF16)
    for hk, kh in enumerate(heads(group(2, True))):
        kaug_ref[0, hk] = jnp.concatenate([onehot, kh.astype(BF16)], axis=1)
    for hk, kh in enumerate(heads(group(4, True))):
        kw_ref[0, hk] = kh.astype(BF16)
    vst = jnp.transpose(group(3, False)).astype(BF16)
    ones_rows = jnp.where(lax.broadcasted_iota(jnp.int32, (V_ROWS - dh, tm), 0) == 0, 1.0, 0.0)
    vwt = jnp.transpose(group(5, False)).astype(BF16)
    for hk in range(NSA_KV_HEADS):
        vst_ref[0, hk, 0] = jnp.concatenate([vst[hk * dh:(hk + 1) * dh], ones_rows.astype(BF16)], axis=0)
        for qb in range(nq):
            vwt_ref[0, hk, qb] = vwt[hk * dh:(hk + 1) * dh, qb * Q_BLOCK:(qb + 1) * Q_BLOCK]

    zg = _dot(h, w_ref[:, NSA_MAIN:NSA_MAIN + GDN_MAIN])
    qkv_ref[...] = zg[:, 0:3 * GDN_WIDTH]
    zg_ref[...] = zg[:, 3 * GDN_WIDTH:]

    zs = _dot(h, w_ref[:, NSA_MAIN + GDN_MAIN:])
    is_raw = (lane >= SMALL_A) & (lane < SMALL_B)
    small = jnp.where(is_raw, zs, _sigmoid(zs))
    small_ref[...] = small
    small_t = jnp.transpose(small)
    for qb in range(nq):
        gt_ref[0, qb] = small_t[0:SMALL_GATE, qb * Q_BLOCK:(qb + 1) * Q_BLOCK]


def _inproj(x, nw, w_packed, cos2, sin2, layer, *, tm, b, seq):
    t, d = x.shape
    nseq = seq // tm
    nq = tm // Q_BLOCK
    nqb = seq // Q_BLOCK
    hkv, dh = NSA_KV_HEADS, NSA_HEAD_DIM
    cols = NSA_GROUP * Q_BLOCK
    row = lambda i: (i, 0)
    tile5 = lambda i: (i // nseq, 0, i % nseq, 0, 0)
    tile4 = lambda i: (i // nseq, 0, i % nseq, 0)
    return pl.pallas_call(
        functools.partial(_inproj_body, nseq=nseq),
        grid=(t // tm,),
        in_specs=[
            pl.BlockSpec((tm, d), row),
            _layer_spec((1, d), layer, lambda i: (0, 0)),
            _layer_spec((d, W_IN_PACKED), layer, lambda i: (0, 0)),
            pl.BlockSpec((tm, LANES), lambda i: (i % nseq, 0)),
            pl.BlockSpec((tm, LANES), lambda i: (i % nseq, 0)),
        ],
        out_specs=[
            pl.BlockSpec((1, hkv, nq, LANES, cols), tile5),
            pl.BlockSpec((1, hkv, tm, 2 * LANES), tile4),
            pl.BlockSpec((1, hkv, 1, V_ROWS, tm), tile5),
            pl.BlockSpec((1, hkv, tm, LANES), tile4),
            pl.BlockSpec((1, hkv, nq, dh, Q_BLOCK), tile5),
            pl.BlockSpec((1, nq, SMALL_GATE, Q_BLOCK), lambda i: (i // nseq, i % nseq, 0, 0)),
            pl.BlockSpec((tm, NSA_KV_WIDTH), row),
            pl.BlockSpec((tm, NSA_KV_WIDTH), row),
            pl.BlockSpec((tm, 3 * GDN_WIDTH), row),
            pl.BlockSpec((tm, GDN_WIDTH), row),
            pl.BlockSpec((tm, LANES), row),
        ],
        out_shape=[
            jax.ShapeDtypeStruct((b, hkv, nqb, LANES, cols), BF16),
            jax.ShapeDtypeStruct((b, hkv, seq, 2 * LANES), BF16),
            jax.ShapeDtypeStruct((b, hkv, nseq, V_ROWS, tm), BF16),
            jax.ShapeDtypeStruct((b, hkv, seq, LANES), BF16),
            jax.ShapeDtypeStruct((b, hkv, nqb, dh, Q_BLOCK), BF16),
            jax.ShapeDtypeStruct((b, nqb, SMALL_GATE, Q_BLOCK), F32),
            jax.ShapeDtypeStruct((t, NSA_KV_WIDTH), F32),
            jax.ShapeDtypeStruct((t, NSA_KV_WIDTH), F32),
            jax.ShapeDtypeStruct((t, 3 * GDN_WIDTH), F32),
            jax.ShapeDtypeStruct((t, GDN_WIDTH), F32),
            jax.ShapeDtypeStruct((t, LANES), F32),
        ],
        compiler_params=_cparams(("parallel",)),
        name="inproj",
    )(x, nw, w_packed, cos2, sin2)


def _compress_body(xk_ref, xv_ref, pe_ref, w1_ref, w2_ref, ok_ref, ov_ref):
    nh = ok_ref.shape[2]
    hkv = NSA_KV_HEADS
    lo = [[jnp.zeros((nh, CMP_HIDDEN), F32) for _ in range(hkv)] for _ in range(2)]
    hi = [[jnp.zeros((nh, CMP_HIDDEN), F32) for _ in range(hkv)] for _ in range(2)]
    for l in range(CMP_STRIDE):
        for kv, x_ref in enumerate((xk_ref, xv_ref)):
            xg = x_ref[0, pl.ds(l, nh, stride=CMP_STRIDE), :]
            x_lo = (xg + pe_ref[kv, l:l + 1, :]).astype(BF16)
            x_hi = (xg + pe_ref[kv, CMP_STRIDE + l:CMP_STRIDE + l + 1, :]).astype(BF16)
            for hk in range(hkv):
                lo[kv][hk] = lo[kv][hk] + _dot(x_lo, w1_ref[kv, hk, l])
                hi[kv][hk] = hi[kv][hk] + _dot(x_hi, w1_ref[kv, hk, CMP_STRIDE + l])
    for hk in range(hkv):
        hid_k = lo[0][hk] + pltpu.roll(hi[0][hk], nh - 1, 0)
        hid_v = lo[1][hk] + pltpu.roll(hi[1][hk], nh - 1, 0)
        ok_ref[0, hk] = _dot(_silu(hid_k).astype(BF16), w2_ref[0]).astype(BF16)
        vt = jnp.transpose(_dot(_silu(hid_v).astype(BF16), w2_ref[1]))
        ov_ref[0, hk] = vt[0:NSA_HEAD_DIM].astype(BF16)


def _compress(xk, xv, pe, w1, w2, layer):
    b, seq, wide = xk.shape
    nh = seq // CMP_STRIDE
    hkv = NSA_KV_HEADS
    return pl.pallas_call(
        _compress_body,
        grid=(b,),
        in_specs=[pl.BlockSpec((1, seq, wide), lambda i: (i, 0, 0)),
                  pl.BlockSpec((1, seq, wide), lambda i: (i, 0, 0)),
                  _layer_spec(pe.shape[1:], layer, lambda i: (0, 0, 0)),
                  _layer_spec(w1.shape[1:], layer, lambda i: (0, 0, 0, 0, 0)),
                  _layer_spec(w2.shape[1:], layer, lambda i: (0, 0, 0))],
        out_specs=[pl.BlockSpec((1, hkv, nh, LANES), lambda i: (i, 0, 0, 0)),
                   pl.BlockSpec((1, hkv, NSA_HEAD_DIM, nh), lambda i: (i, 0, 0, 0))],
        out_shape=[jax.ShapeDtypeStruct((b, hkv, nh, LANES), BF16),
                   jax.ShapeDtypeStruct((b, hkv, NSA_HEAD_DIM, nh), BF16)],
        compiler_params=_cparams(("parallel",)),
        name="compress",
    )(xk, xv, pe, w1, w2)


def _colmax(s):
    r = s.shape[0]
    while r > 8 and r % 4 == 0:
        r //= 4
        s = jnp.max(s.reshape(4, r, s.shape[-1]), axis=0)
    return jnp.max(s, axis=0, keepdims=True)


def _nsa_body(q_ref, kc_ref, vc_ref, kaug_ref, vs_ref, kw_ref, vw_ref, g_ref, ovt_ref,
              o_ref, s_ref, *, n_slc, n_sel, tk):
    i = pl.program_id(2)
    t0 = i * Q_BLOCK
    cols = NSA_GROUP * Q_BLOCK
    dh = NSA_HEAD_DIM
    q_t = q_ref[0, 0, 0]
    lane = lax.broadcasted_iota(jnp.int32, (1, cols), 1)
    tq = t0 + (lane & (Q_BLOCK - 1))

    def softmax_cols(s, mask):
        s = jnp.where(mask, s, NEG_INF)
        m = _colmax(s)
        e = jnp.where(mask, jnp.exp2(s - m), 0.0)
        l = jnp.sum(e, axis=0, keepdims=True)
        return e, 1.0 / jnp.where(l > 0.0, l, 1.0)

    kc = kc_ref[0, 0]
    nc = kc.shape[0]
    crow = lax.broadcasted_iota(jnp.int32, (nc, 1), 0)
    e_c, inv_c = softmax_cols(_dot(kc, q_t), (crow * CMP_STRIDE + (CMP_BLOCK - 1)) <= tq)
    o_cmp = _dot(vc_ref[0, 0], e_c.astype(BF16)) * inv_c

    p_c = e_c * inv_c
    pg = p_c[:, 0:Q_BLOCK]
    for g in range(1, NSA_GROUP):
        pg = pg + p_c[:, g * Q_BLOCK:(g + 1) * Q_BLOCK]
    imp = _dot_exact_lhs(ovt_ref[...], pg)

    blk = lax.broadcasted_iota(jnp.int32, (LANES, Q_BLOCK), 0)
    tcol = t0 + lax.broadcasted_iota(jnp.int32, (LANES, Q_BLOCK), 1)
    cur = jnp.right_shift(tcol, SLC_BLOCK.bit_length() - 1)
    forced = (blk == 0) | (blk == cur) | (blk == cur - 1)
    valid = blk * SLC_BLOCK <= tcol
    score = jnp.where(forced, FORCE_SCORE, jnp.where(valid, imp, -1.0))
    score = jnp.where(blk < n_slc, score, -jnp.inf)
    bias = jnp.full((LANES, Q_BLOCK), SEL_BIAS, F32)
    for _ in range(n_sel):
        mx = jnp.max(score, axis=0, keepdims=True)
        idx = jnp.min(jnp.where(score == mx, blk, 2 * LANES), axis=0, keepdims=True)
        hit = blk == idx
        bias = jnp.where(hit, 0.0, bias)
        score = jnp.where(hit, -jnp.inf, score)
    bias = bias.astype(BF16)
    q_aug = jnp.concatenate([jnp.concatenate([bias] * NSA_GROUP, axis=1), q_t], axis=0)

    def scores(j):
        return _dot(kaug_ref[0, 0, pl.ds(pl.multiple_of(j * tk, tk), tk), :], q_aug)

    def absorb(j, slot, m, acc, causal):
        s = s_ref[slot]
        if causal:
            kpos = j * tk + lax.broadcasted_iota(jnp.int32, (tk, 1), 0)
            s = jnp.where(kpos <= tq, s, NEG_INF)
        m_new = jnp.maximum(m, _colmax(s))
        p = jnp.exp2(s - m_new).astype(BF16)
        return m_new, jnp.exp2(m - m_new) * acc + _dot(vs_ref[0, 0, j], p)

    def slc_pair(jj, carry):
        m, acc = carry
        j = 2 * jj
        s_ref[1] = scores(j + 1)
        m, acc = absorb(j, 0, m, acc, False)
        s_ref[0] = scores(j + 2)
        return absorb(j + 1, 1, m, acc, False)

    def tail_odd(m, acc):
        s_ref[1] = scores(n_full)
        m, acc = absorb(n_full - 1, 0, m, acc, False)
        return absorb(n_full, 1, m, acc, True)

    def tail_even(m, acc):
        return absorb(n_full, 0, m, acc, True)

    n_full = t0 // tk
    s_ref[0] = scores(0)
    init = (jnp.full((1, cols), M_INIT, F32), jnp.zeros((V_ROWS, cols), F32))
    m_s, acc_s = lax.fori_loop(0, n_full // 2, slc_pair, init)
    _, acc_s = lax.cond(n_full % 2 == 1, tail_odd, tail_even, m_s, acc_s)
    o_slc = acc_s[0:dh] * (1.0 / acc_s[dh:dh + 1])

    wspan = Q_BLOCK + WINDOW
    c0 = jnp.maximum(i - WINDOW // Q_BLOCK, 0)
    start = pl.multiple_of(c0 * Q_BLOCK, Q_BLOCK)
    kpos = start + lax.broadcasted_iota(jnp.int32, (wspan, 1), 0)
    dist = tq - kpos
    e_w, inv_w = softmax_cols(_dot(kw_ref[0, 0, pl.ds(start, wspan), :], q_t),
                              (dist >= 0) & (dist < WINDOW))
    e_w = e_w.astype(BF16)
    o_win = _dot(vw_ref[0, 0, c0], e_w[0:Q_BLOCK])
    for c in range(1, wspan // Q_BLOCK):
        o_win = o_win + _dot(vw_ref[0, 0, c0 + c], e_w[c * Q_BLOCK:(c + 1) * Q_BLOCK])
    o_win = o_win * inv_w

    gate = g_ref[0, 0]
    outs = []
    for g in range(NSA_GROUP):
        sl = slice(g * Q_BLOCK, (g + 1) * Q_BLOCK)
        outs.append(gate[3 * g:3 * g + 1] * o_cmp[:, sl] + gate[3 * g + 1:3 * g + 2] * o_slc[:, sl]
                    + gate[3 * g + 2:3 * g + 3] * o_win[:, sl])
    pairs = [jnp.transpose(jnp.concatenate(outs[2 * k:2 * k + 2], axis=0)) for k in range(NSA_GROUP // 2)]
    o_ref[0] = jnp.concatenate(pairs, axis=1).astype(BF16)


def _nsa(q_t, kcmp, vcmp_t, kaug, vs_t, kw, vw_t, gates_t, ovt):
    b, hkv, nqb, _, cols = q_t.shape
    seq = kaug.shape[2]
    tk = vs_t.shape[4]
    n_slc = seq // SLC_BLOCK
    per_head = lambda a: pl.BlockSpec((1, 1) + a.shape[2:],
                                      lambda bi, hi, qi: (bi, hi) + (0,) * (a.ndim - 2))
    body = functools.partial(_nsa_body, n_slc=n_slc, n_sel=min(N_SELECTED, n_slc), tk=tk)
    return pl.pallas_call(
        body,
        grid=(b, hkv, nqb),
        in_specs=[pl.BlockSpec((1, 1, 1) + q_t.shape[3:], lambda bi, hi, qi: (bi, hi, qi, 0, 0)),
                  per_head(kcmp), per_head(vcmp_t), per_head(kaug), per_head(vs_t),
                  per_head(kw), per_head(vw_t),
                  pl.BlockSpec((1, 1, GATE_ROWS, Q_BLOCK), lambda bi, hi, qi: (bi, qi, hi, 0)),
                  pl.BlockSpec(ovt.shape, lambda bi, hi, qi: (0, 0))],
        out_specs=pl.BlockSpec((1, Q_BLOCK, NSA_GROUP * NSA_HEAD_DIM), lambda bi, hi, qi: (bi, qi, hi)),
        out_shape=jax.ShapeDtypeStruct((b, seq, NSA_WIDTH), BF16),
        scratch_shapes=[pltpu.VMEM((2, tk, cols), F32)],
        compiler_params=_cparams(("parallel", "parallel", "arbitrary")),
        name="nsa",
    )(q_t, kcmp, vcmp_t, kaug, vs_t, kw, vw_t, gates_t, ovt)


GDN_TILE = 2 * GDN_CHUNK


def _gdn_prep_body(qkv_ref, halo_ref, cw_ref, small_ref, alog_ref, dtb_ref,
                   u_ref, wq_ref, kdt_ref, attn_ref, eg_ref, *, ts):
    i = pl.program_id(1)
    x = qkv_ref[0]
    halo = jnp.where(i > 0, halo_ref[0], 0.0)
    xx = jnp.concatenate([halo, x], axis=0)
    y = x * cw_ref[CONV_WIDTH - 1:CONV_WIDTH, :]
    for d in range(1, CONV_WIDTH):
        shifted = pltpu.roll(xx, d, 0)[8:]
        y = y + shifted * cw_ref[CONV_WIDTH - 1 - d:CONV_WIDTH - d, :]
    y = _silu(y)

    sm = small_ref[0]
    sp_in = sm + dtb_ref[...]
    softplus = jnp.maximum(sp_in, 0.0) + jnp.log(1.0 + jnp.exp(-jnp.abs(sp_in)))
    glog = -jnp.exp(alog_ref[...]) * softplus

    ri = lax.broadcasted_iota(jnp.int32, (ts, ts), 0)
    ci = lax.broadcasted_iota(jnp.int32, (ts, ts), 1)
    sh = GDN_CHUNK.bit_length() - 1
    same = jnp.right_shift(ri, sh) == jnp.right_shift(ci, sh)
    tril = jnp.where(same & (ri >= ci), 1.0, 0.0).astype(BF16)
    ones = jnp.where(same, 1.0, 0.0).astype(BF16)
    gcum = _dot_exact_lhs(tril, glog)
    glast = _dot_exact_lhs(ones, glog)

    r2 = lax.broadcasted_iota(jnp.int32, (GDN_TILE, GDN_TILE), 0)
    c2 = lax.broadcasted_iota(jnp.int32, (GDN_TILE, GDN_TILE), 1)
    same2 = jnp.right_shift(r2, sh) == jnp.right_shift(c2, sh)
    incl = same2 & (r2 >= c2)
    strict = same2 & (r2 > c2)
    eye = jnp.where(r2 == c2, 1.0, 0.0)
    qscale = GDN_HEAD_DIM ** -0.5

    units = []
    for c in range(ts // GDN_TILE):
        r0 = c * GDN_TILE
        gc_tile = gcum[r0:r0 + GDN_TILE]
        gc_rows = jnp.transpose(gc_tile)
        for h in range(GDN_HEADS):
            lo = h * GDN_HEAD_DIM
            qh = y[r0:r0 + GDN_TILE, lo:lo + GDN_HEAD_DIM]
            kh = y[r0:r0 + GDN_TILE, GDN_WIDTH + lo:GDN_WIDTH + lo + GDN_HEAD_DIM]
            vh = y[r0:r0 + GDN_TILE, 2 * GDN_WIDTH + lo:2 * GDN_WIDTH + lo + GDN_HEAD_DIM]
            qh = qh * lax.rsqrt(jnp.sum(qh * qh, axis=-1, keepdims=True) + EPS)
            kh = kh * lax.rsqrt(jnp.sum(kh * kh, axis=-1, keepdims=True) + EPS)
            gc_col = gc_tile[:, SMALL_A + h:SMALL_A + h + 1]
            gc_row = gc_rows[SMALL_A + h:SMALL_A + h + 1, :]
            gl_col = glast[r0:r0 + GDN_TILE, SMALL_A + h:SMALL_A + h + 1]
            beta = sm[r0:r0 + GDN_TILE, SMALL_B + h:SMALL_B + h + 1]

            decay = jnp.where(incl, jnp.exp(jnp.minimum(gc_col - gc_row, 0.0)), 0.0)
            kb = kh * beta
            k_bf = kh.astype(BF16)
            a_s = jnp.where(strict, _dot_nt(kb.astype(BF16), k_bf) * decay, 0.0)
            egc = jnp.exp(gc_col)
            qs = qh * qscale
            attn = jnp.where(incl, _dot_nt(qs.astype(BF16), k_bf) * decay, 0.0)
            attn_ref[0, h, r0:r0 + GDN_TILE, :] = attn.astype(BF16)
            k_dec = kh * jnp.exp(gl_col - gc_col)
            kdt_ref[0, h, r0:r0 + GDN_TILE, :] = jnp.transpose(k_dec).astype(BF16)
            for cc in range(2):
                e0 = (r0 // GDN_CHUNK + cc) * 8
                eg_ref[0, h, e0:e0 + 8, :] = jnp.broadcast_to(
                    jnp.exp(gl_col[cc * GDN_CHUNK:cc * GDN_CHUNK + 8]), (8, GDN_HEAD_DIM))
            units.append(dict(h=h, r0=r0, a=a_s, rhs_u=vh * beta, rhs_w=kb * egc, q_dec=qs * egc))

    xinv = [eye - un['a'] for un in units]
    asp = [_split2(un['a']) for un in units]
    pw = [_dot_split(a, a) for a in asp]
    steps = GDN_CHUNK.bit_length() - 2
    for s in range(steps):
        pws = [_split2(p) for p in pw]
        xinv = [x + _dot_split(_split2(x), p) for x, p in zip(xinv, pws)]
        if s + 1 < steps:
            pw = [_dot_split(p, p) for p in pws]

    for un, x in zip(units, xinv):
        h, r0 = un['h'], un['r0']
        xs = _split2(x)
        u = _dot_split(xs, _split2(un['rhs_u']))
        w = _dot_split(xs, _split2(un['rhs_w']))
        u_ref[0, h, r0:r0 + GDN_TILE, :] = u
        for cc in range(2):
            a0 = cc * GDN_CHUNK
            wq = jnp.concatenate([w[a0:a0 + GDN_CHUNK], un['q_dec'][a0:a0 + GDN_CHUNK]], axis=0)
            n0 = 2 * r0 + cc * GDN_TILE
            wq_ref[0, h, n0:n0 + GDN_TILE, :] = wq.astype(BF16)


def _gdn_prep(qkv, cw, small, alog_row, dtb_row, layer, *, ts):
    b, seq, wide = qkv.shape
    nt = seq // ts
    hd = GDN_HEAD_DIM
    hspec = lambda rows: pl.BlockSpec((1, GDN_HEADS, rows, hd), lambda bi, ti: (bi, 0, ti, 0))
    hshape = lambda rows, dt: jax.ShapeDtypeStruct((b, GDN_HEADS, rows, hd), dt)
    return pl.pallas_call(
        functools.partial(_gdn_prep_body, ts=ts),
        grid=(b, nt),
        in_specs=[
            pl.BlockSpec((1, ts, wide), lambda bi, ti: (bi, ti, 0)),
            pl.BlockSpec((1, 8, wide), lambda bi, ti: (bi, jnp.maximum(ti * (ts // 8) - 1, 0), 0)),
            _layer_spec(cw.shape[1:], layer, lambda bi, ti: (0, 0)),
            pl.BlockSpec((1, ts, LANES), lambda bi, ti: (bi, ti, 0)),
            _layer_spec((1, LANES), layer, lambda bi, ti: (0, 0)),
            _layer_spec((1, LANES), layer, lambda bi, ti: (0, 0)),
        ],
        out_specs=[hspec(ts), hspec(2 * ts), hspec(ts), hspec(ts), hspec(ts // 8)],
        out_shape=[hshape(seq, F32), hshape(2 * seq, BF16), hshape(seq, BF16),
                   hshape(seq, BF16), hshape(seq // 8, F32)],
        compiler_params=_cparams(("parallel", "parallel")),
        name="gdn_prep",
    )(qkv, qkv, cw, small, alog_row, dtb_row)


def _gdn_scan_body(u_ref, wq_ref, kdt_ref, attn_ref, eg_ref, zg_ref, gn_ref, o_ref, st_ref, *, ts):
    @pl.when(pl.program_id(1) == 0)
    def _():
        st_ref[...] = jnp.zeros_like(st_ref)

    hd = GDN_HEAD_DIM
    gn = gn_ref[...]
    zeros = jnp.zeros((GDN_CHUNK, hd), F32)
    for h in range(GDN_HEADS):
        state = st_ref[h]
        for n in range(ts // GDN_CHUNK):
            r0 = n * GDN_CHUNK
            t0 = (n // 2) * GDN_TILE
            r = _dot(wq_ref[0, h, 2 * r0:2 * r0 + GDN_TILE, :], state.astype(BF16))
            v_new = u_ref[0, h, r0:r0 + GDN_CHUNK, :] - r[0:GDN_CHUNK]
            vpad = jnp.concatenate([v_new, zeros] if n % 2 == 0 else [zeros, v_new], axis=0)
            vpad = vpad.astype(BF16)
            o = r[GDN_CHUNK:] + _dot(attn_ref[0, h, r0:r0 + GDN_CHUNK, :], vpad)
            state = state * eg_ref[0, h, 8 * n:8 * n + 1, :] + _dot(kdt_ref[0, h, t0:t0 + GDN_TILE, :], vpad)
            on = o * lax.rsqrt(jnp.mean(o * o, axis=-1, keepdims=True) + EPS) * gn
            gate = _silu(zg_ref[0, r0:r0 + GDN_CHUNK, h * hd:(h + 1) * hd])
            o_ref[0, r0:r0 + GDN_CHUNK, h * hd:(h + 1) * hd] = (on * gate).astype(BF16)
        st_ref[h] = state


def _gdn_scan(u, wq, kdt, attn, eg, zg, gn, layer, *, ts):
    b, nh, seq, hd = u.shape
    hspec = lambda rows: pl.BlockSpec((1, nh, rows, hd), lambda bi, ti: (bi, 0, ti, 0))
    return pl.pallas_call(
        functools.partial(_gdn_scan_body, ts=ts),
        grid=(b, seq // ts),
        in_specs=[hspec(ts), hspec(2 * ts), hspec(ts), hspec(ts), hspec(ts // 8),
                  pl.BlockSpec((1, ts, nh * hd), lambda bi, ti: (bi, ti, 0)),
                  _layer_spec((1, hd), layer, lambda bi, ti: (0, 0))],
        out_specs=pl.BlockSpec((1, ts, nh * hd), lambda bi, ti: (bi, ti, 0)),
        out_shape=jax.ShapeDtypeStruct((b, seq, nh * hd), BF16),
        scratch_shapes=[pltpu.VMEM((nh, hd, hd), F32)],
        compiler_params=_cparams(("parallel", "arbitrary")),
        name="gdn_scan",
    )(u, wq, kdt, attn, eg, zg, gn)


def _outproj_body(x_ref, a_ref, b_ref, w_ref, o_ref):
    half = a_ref.shape[1]
    o_ref[...] = x_ref[...] + _dot(a_ref[...], w_ref[0:half, :]) + _dot(b_ref[...], w_ref[half:, :])


def _outproj(x, o_nsa, o_gdn, w_out, layer, *, tm):
    t, d = x.shape
    row = lambda i: (i, 0)
    return pl.pallas_call(
        _outproj_body,
        grid=(t // tm,),
        in_specs=[pl.BlockSpec((tm, d), row), pl.BlockSpec((tm, o_nsa.shape[1]), row),
                  pl.BlockSpec((tm, o_gdn.shape[1]), row),
                  _layer_spec(w_out.shape[1:], layer, lambda i: (0, 0))],
        out_specs=pl.BlockSpec((tm, d), row),
        out_shape=jax.ShapeDtypeStruct((t, d), F32),
        compiler_params=_cparams(("parallel",)),
        name="outproj",
    )(x, o_nsa, o_gdn, w_out)


def _ple_body(x_ref, p_ref, nw_ref, wg_ref, wp_ref, fn_ref, o_ref, *, final):
    x = x_ref[...]
    h = _rms(x, nw_ref[...]).astype(BF16)
    gate = _sigmoid(_dot(h, wg_ref[...]))
    out = x + gate * _dot(p_ref[...].astype(BF16), wp_ref[...])
    if final:
        out = _rms(out, fn_ref[...])
    o_ref[...] = out


def _ple(x, p, nw, wg, wp, fn, layer, *, tm, final):
    t, d = x.shape
    row = lambda i: (i, 0)
    return pl.pallas_call(
        functools.partial(_ple_body, final=final),
        grid=(t // tm,),
        in_specs=[pl.BlockSpec((tm, d), row), _layer_spec((tm, p.shape[2]), layer, row),
                  _layer_spec((1, d), layer, lambda i: (0, 0)),
                  _layer_spec(wg.shape[1:], layer, lambda i: (0, 0)),
                  _layer_spec(wp.shape[1:], layer, lambda i: (0, 0)),
                  pl.BlockSpec(fn.shape, lambda i: (0, 0))],
        out_specs=pl.BlockSpec((tm, d), row),
        out_shape=jax.ShapeDtypeStruct((t, d), F32),
        compiler_params=_cparams(("parallel",)),
        name="ple",
    )(x, p, nw, wg, wp, fn)


def _pack_w_in(w_in):
    offs = [0]
    for s in IN_SIZES:
        offs.append(offs[-1] + s)
    main = w_in[..., offs[0]:offs[7]]
    gates = w_in[..., offs[7]:offs[8]]
    gdn = w_in[..., offs[8]:offs[10]]
    ab = w_in[..., offs[10]:offs[12]]
    per_head = 3 * NSA_GROUP
    zpad = lambda n: jnp.zeros(w_in.shape[:-1] + (n,), w_in.dtype)
    gate_cols = []
    for hk in range(NSA_KV_HEADS):
        gate_cols += [gates[..., hk * per_head:(hk + 1) * per_head], zpad(GATE_ROWS - per_head)]
    tail = zpad(LANES - SMALL_GATE - ab.shape[-1])
    return jnp.concatenate([main, gdn] + gate_cols + [ab, tail], axis=-1).astype(BF16)


def _pack_cmp_w1(w1):
    depth = w1.shape[0]
    w1r = w1.reshape(depth, CMP_BLOCK, NSA_HEAD_DIM, CMP_HIDDEN).astype(BF16)
    z = jnp.zeros_like(w1r)
    return jnp.stack([jnp.concatenate([w1r, z], axis=2), jnp.concatenate([z, w1r], axis=2)], axis=1)


def _rope_tables(seq):
    dim = NSA_HEAD_DIM
    inv = 1.0 / (ROPE_THETA ** (jnp.arange(0, dim, 2, dtype=F32) / dim))
    ang = jnp.arange(seq, dtype=F32)[:, None] * inv[None, :]
    ang = jnp.concatenate([ang, ang], axis=-1)
    cos, sin = jnp.cos(ang), jnp.sin(ang)
    sign = jnp.where(jnp.arange(dim) < dim // 2, -1.0, 1.0).astype(F32)
    return jnp.tile(cos, (1, LANES // dim)), jnp.tile(sin * sign[None, :], (1, LANES // dim))


def _overlap_t(seq, nc_pad):
    n_slc = seq // SLC_BLOCK
    jc = jnp.arange(nc_pad)[None, :]
    js = jnp.arange(LANES)[:, None]
    ov = ((jc * CMP_STRIDE < (js + 1) * SLC_BLOCK) & (jc * CMP_STRIDE + CMP_BLOCK > js * SLC_BLOCK)
          & (js < n_slc))
    return ov.astype(BF16)


def _layer(x2, layer, w, consts, *, b, seq, final, cfg):
    t = b * seq
    x2 = _ffn(x2, w['ffn1_norm'], w['ffn1_w1'], w['ffn1_w3'], w['ffn1_w2'], layer,
              tm=cfg['ffn_tm'], tf=cfg['ffn_tf'])
    (q_t, kaug, vs_t, kw, vw_t, gates_t, cmpk, cmpv, qkv, zg, small) = _inproj(
        x2, w['mix_norm'], w['w_in'], consts['cos'], consts['sin'], layer, tm=cfg['in_tm'], b=b, seq=seq)
    kcmp, vcmp_t = _compress(cmpk.reshape(b, seq, LANES), cmpv.reshape(b, seq, LANES), w['cmp_pe'],
                             w['cmp_w1'], w['cmp_w2'], layer)
    o_nsa = _nsa(q_t, kcmp, vcmp_t, kaug, vs_t, kw, vw_t, gates_t, consts['ovt'])
    u, wq, kdt, attn, eg = _gdn_prep(qkv.reshape(b, seq, 3 * GDN_WIDTH), w['gdn_conv'],
                                     small.reshape(b, seq, LANES), w['gdn_a_log'], w['gdn_dt_bias'],
                                     layer, ts=cfg['prep_ts'])
    o_gdn = _gdn_scan(u, wq, kdt, attn, eg, zg.reshape(b, seq, GDN_WIDTH), w['gdn_norm'], layer,
                      ts=cfg['scan_ts'])
    x2 = _outproj(x2, o_nsa.reshape(t, NSA_WIDTH), o_gdn.reshape(t, GDN_WIDTH), w['w_out'], layer,
                  tm=cfg['out_tm'])
    x2 = _ffn(x2, w['ffn2_norm'], w['ffn2_w1'], w['ffn2_w3'], w['ffn2_w2'], layer,
              tm=cfg['ffn_tm'], tf=cfg['ffn_tf'])
    return _ple(x2, w['p'], w['ple_norm'], w['ple_gate'], w['ple_proj'], consts['final_norm'], layer,
                tm=cfg['ple_tm'], final=final)


DEFAULT_CFG = dict(ffn_tm=512, ffn_tf=1408, in_tm=512, prep_ts=256, scan_ts=256,
                   out_tm=512, ple_tm=512)


def _forward(x, p, w, cfg):
    b, seq, d = x.shape
    depth = p.shape[0]
    t = b * seq
    cos2, sin2 = _rope_tables(seq)
    consts = dict(cos=cos2, sin=sin2, ovt=_overlap_t(seq, seq // CMP_STRIDE),
                  final_norm=w['final_norm'].reshape(1, d))
    bf = lambda a: a.astype(BF16)
    row3 = lambda a: a.reshape(depth, 1, a.shape[-1])
    lane_rows = lambda v, off: jnp.zeros((depth, 1, LANES), F32).at[:, 0, off:off + v.shape[1]].set(v)
    pe = jnp.stack([w['cmp_pe_k'], w['cmp_pe_v']], axis=1)
    w2 = jnp.stack([w['cmp_k_w2'], w['cmp_v_w2']], axis=1)
    ws = dict(
        p=p.reshape(depth, t, p.shape[-1]),
        ffn1_norm=row3(w['ffn1_norm']), ffn1_w1=bf(w['ffn1_w1']), ffn1_w3=bf(w['ffn1_w3']),
        ffn1_w2=bf(w['ffn1_w2']),
        mix_norm=row3(w['mix_norm']), w_in=_pack_w_in(w['w_in']),
        cmp_pe=jnp.concatenate([pe, pe], axis=-1),
        cmp_w1=jnp.stack([_pack_cmp_w1(w['cmp_k_w1']), _pack_cmp_w1(w['cmp_v_w1'])], axis=1),
        cmp_w2=bf(jnp.pad(w2, ((0, 0), (0, 0), (0, 0), (0, LANES - w2.shape[-1])))),
        gdn_conv=w['gdn_conv'], gdn_a_log=lane_rows(w['gdn_a_log'], SMALL_A),
        gdn_dt_bias=lane_rows(w['gdn_dt_bias'], SMALL_A), gdn_norm=row3(w['gdn_norm']),
        w_out=bf(w['w_out']),
        ffn2_norm=row3(w['ffn2_norm']), ffn2_w1=bf(w['ffn2_w1']), ffn2_w3=bf(w['ffn2_w3']),
        ffn2_w2=bf(w['ffn2_w2']),
        ple_norm=row3(w['ple_norm']), ple_gate=bf(w['ple_gate']), ple_proj=bf(w['ple_proj']),
    )
    x2 = x.reshape(t, d)
    for i in range(depth):
        x2 = _layer(x2, i, ws, consts, b=b, seq=seq, final=(i == depth - 1), cfg=cfg)
    return x2.reshape(b, seq, d)


def kernel(x, p, ffn1_norm, ffn1_w1, ffn1_w3, ffn1_w2, mix_norm, w_in, cmp_pe_k, cmp_pe_v,
           cmp_k_w1, cmp_k_w2, cmp_v_w1, cmp_v_w2, gdn_conv, gdn_a_log, gdn_dt_bias, gdn_norm,
           w_out, ffn2_norm, ffn2_w1, ffn2_w3, ffn2_w2, ple_norm, ple_gate, ple_proj, final_norm):
    w = dict(ffn1_norm=ffn1_norm, ffn1_w1=ffn1_w1, ffn1_w3=ffn1_w3, ffn1_w2=ffn1_w2,
             mix_norm=mix_norm, w_in=w_in, cmp_pe_k=cmp_pe_k, cmp_pe_v=cmp_pe_v,
             cmp_k_w1=cmp_k_w1, cmp_k_w2=cmp_k_w2, cmp_v_w1=cmp_v_w1, cmp_v_w2=cmp_v_w2,
             gdn_conv=gdn_conv, gdn_a_log=gdn_a_log, gdn_dt_bias=gdn_dt_bias, gdn_norm=gdn_norm,
             w_out=w_out, ffn2_norm=ffn2_norm, ffn2_w1=ffn2_w1, ffn2_w3=ffn2_w3, ffn2_w2=ffn2_w2,
             ple_norm=ple_norm, ple_gate=ple_gate, ple_proj=ple_proj, final_norm=final_norm)
    return _forward(x, p, w, DEFAULT_CFG)
```

```python
import functools

import jax
import jax.numpy as jnp
from jax import lax
from jax.experimental import pallas as pl
from jax.experimental.pallas import tpu as pltpu

F32 = jnp.float32
BF16 = jnp.bfloat16

D_MODEL = 1024
NSA_HEADS = 8
NSA_KV_HEADS = 2
NSA_GROUP = NSA_HEADS // NSA_KV_HEADS
NSA_HEAD_DIM = 64
CMP_BLOCK = 32
CMP_STRIDE = 16
CMP_HIDDEN = 128
SLC_BLOCK = 64
N_SELECTED = 16
WINDOW = 512
Q_BLOCK = 128
GDN_HEADS = 4
GDN_HEAD_DIM = 128
GDN_CHUNK = 64
CONV_WIDTH = 4
D_FF = 2816
PLE_DIM = 256
ROPE_THETA = 10000.0
EPS = 1e-6
FORCE_SCORE = 1e6
NEG_INF = -1e30

NSA_WIDTH = NSA_HEADS * NSA_HEAD_DIM
NSA_KV_WIDTH = NSA_KV_HEADS * NSA_HEAD_DIM
GDN_WIDTH = GDN_HEADS * GDN_HEAD_DIM
IN_SIZES = (NSA_WIDTH, NSA_KV_WIDTH, NSA_KV_WIDTH, NSA_KV_WIDTH, NSA_KV_WIDTH,
            NSA_KV_WIDTH, NSA_KV_WIDTH, 3 * NSA_HEADS, 3 * GDN_WIDTH, GDN_WIDTH,
            GDN_HEADS, GDN_HEADS)

LANES = 128
NSA_MAIN = NSA_WIDTH + 6 * NSA_KV_WIDTH
GDN_MAIN = 4 * GDN_WIDTH
GATE_ROWS = 16
SMALL_GATE = NSA_KV_HEADS * GATE_ROWS
SMALL_A = SMALL_GATE
SMALL_B = SMALL_GATE + GDN_HEADS
W_IN_PACKED = NSA_MAIN + GDN_MAIN + LANES
V_ROWS = NSA_HEAD_DIM + 16
SEL_BIAS = -2.0 ** 100
M_INIT = -3.0e38
LOG2_E = 1.4426950408889634
VMEM_LIMIT = 56 * 1024 * 1024


def _cparams(sem):
    return pltpu.CompilerParams(dimension_semantics=sem, vmem_limit_bytes=VMEM_LIMIT)


def _rms(x, w):
    ms = jnp.mean(x * x, axis=-1, keepdims=True)
    return x * lax.rsqrt(ms + EPS) * w


def _sigmoid(x):
    return 1.0 / (1.0 + jnp.exp(-x))


def _silu(x):
    return x * _sigmoid(x)


def _dot(a, b):
    return jnp.dot(a, b, preferred_element_type=F32)


def _dot_nt(a, b):
    return lax.dot_general(a, b, (((1,), (1,)), ((), ())), preferred_element_type=F32)


def _split3(x):
    hi = x.astype(BF16)
    r = x - hi.astype(F32)
    mid = r.astype(BF16)
    lo = (r - mid.astype(F32)).astype(BF16)
    return hi, mid, lo


def _dot_exact_lhs(a_bf, x):
    hi, mid, lo = _split3(x)
    return _dot(a_bf, hi) + (_dot(a_bf, mid) + _dot(a_bf, lo))


def _split2(x):
    hi = x.astype(BF16)
    return hi, (x - hi.astype(F32)).astype(BF16)


def _dot_split(a, b):
    ah, am = a
    bh, bm = b
    return _dot(ah, bh) + (_dot(ah, bm) + _dot(am, bh))


def _layer_spec(block, layer, index):
    return pl.BlockSpec((None,) + block, lambda *g: (layer,) + index(*g))


def _ffn_body(x_ref, nw_ref, w1_ref, w3_ref, w2_ref, o_ref, h_ref, acc_ref):
    j = pl.program_id(1)

    @pl.when(j == 0)
    def _():
        h_ref[...] = _rms(x_ref[...], nw_ref[...]).astype(BF16)
        acc_ref[...] = jnp.zeros_like(acc_ref)

    h = h_ref[...]
    u = _dot(h, w1_ref[...])
    g = _dot(h, w3_ref[...])
    a = (_silu(u) * g).astype(BF16)
    acc_ref[...] += _dot(a, w2_ref[...])

    @pl.when(j == pl.num_programs(1) - 1)
    def _():
        o_ref[...] = x_ref[...] + 0.5 * acc_ref[...]


def _ffn(x, nw, w1, w3, w2, layer, *, tm, tf):
    t, d = x.shape
    ff = w1.shape[2]
    return pl.pallas_call(
        _ffn_body,
        grid=(t // tm, ff // tf),
        in_specs=[
            pl.BlockSpec((tm, d), lambda i, j: (i, 0)),
            _layer_spec((1, d), layer, lambda i, j: (0, 0)),
            _layer_spec((d, tf), layer, lambda i, j: (0, j)),
            _layer_spec((d, tf), layer, lambda i, j: (0, j)),
            _layer_spec((tf, d), layer, lambda i, j: (j, 0)),
        ],
        out_specs=pl.BlockSpec((tm, d), lambda i, j: (i, 0)),
        out_shape=jax.ShapeDtypeStruct((t, d), F32),
        scratch_shapes=[pltpu.VMEM((tm, d), BF16), pltpu.VMEM((tm, d), F32)],
        compiler_params=_cparams(("parallel", "arbitrary")),
        name="ffn",
    )(x, nw, w1, w3, w2)


def _rope(xg, cos, sin_signed, first_half):
    fwd = pltpu.roll(xg, LANES - NSA_HEAD_DIM // 2, 1)
    bwd = pltpu.roll(xg, NSA_HEAD_DIM // 2, 1)
    return xg * cos + jnp.where(first_half, fwd, bwd) * sin_signed


def _inproj_body(x_ref, nw_ref, w_ref, cos_ref, sin_ref,
                 qt_ref, kaug_ref, vst_ref, kw_ref, vwt_ref, gt_ref,
                 cmpk_ref, cmpv_ref, qkv_ref, zg_ref, small_ref, *, nseq):
    tm = x_ref.shape[0]
    nq = tm // Q_BLOCK
    dh = NSA_HEAD_DIM
    h = _rms(x_ref[...], nw_ref[...]).astype(BF16)
    cos = cos_ref[...]
    sin_s = sin_ref[...]
    lane = lax.broadcasted_iota(jnp.int32, (1, LANES), 1)
    first_half = (lane & (dh - 1)) < (dh // 2)
    low = lane < dh

    z = _dot(h, w_ref[:, 0:NSA_MAIN])
    scale = dh ** -0.5 * LOG2_E
    for pair in range(NSA_WIDTH // LANES):
        zq = _rope(z[:, pair * LANES:(pair + 1) * LANES], cos, sin_s, first_half) * scale
        tr = jnp.transpose(zq).astype(BF16)
        for half in range(2):
            hk, g = divmod(2 * pair + half, NSA_GROUP)
            for qb in range(nq):
                qt_ref[0, hk, qb, 0:dh, g * Q_BLOCK:(g + 1) * Q_BLOCK] = (
                    tr[half * dh:(half + 1) * dh, qb * Q_BLOCK:(qb + 1) * Q_BLOCK])
    qt_ref[0, :, :, dh:, :] = jnp.zeros((NSA_KV_HEADS, nq, LANES - dh, NSA_GROUP * Q_BLOCK), BF16)

    def group(c, rotary):
        zc = z[:, NSA_WIDTH + c * LANES: NSA_WIDTH + (c + 1) * LANES]
        return _rope(zc, cos, sin_s, first_half) if rotary else zc

    def heads(zc):
        return [jnp.where(low, zc, 0.0), jnp.where(low, pltpu.roll(zc, dh, 1), 0.0)]

    cmpk_ref[...] = group(0, True)
    cmpv_ref[...] = group(1, False)
    tok = (pl.program_id(0) % nseq) * tm + lax.broadcasted_iota(jnp.int32, (tm, 1), 0)
    local_blk = jnp.right_shift(tok, SLC_BLOCK.bit_length() - 1) & (tm // SLC_BLOCK - 1)
    onehot = jnp.where(lane - dh == local_blk, 1.0, 0.0)
    for hk, kh in enumerate(heads(group(2, True))):
        kaug_ref[0, hk] = jnp.where(low, kh, onehot).astype(BF16)
    for hk, kh in enumerate(heads(group(4, True))):
        kw_ref[0, hk] = kh.astype(BF16)
    vst = jnp.transpose(group(3, False)).astype(BF16)
    ones_rows = jnp.where(lax.broadcasted_iota(jnp.int32, (V_ROWS - dh, tm), 0) == 0, 1.0, 0.0)
    vwt = jnp.transpose(group(5, False)).astype(BF16)
    for hk in range(NSA_KV_HEADS):
        vst_ref[0, hk, 0] = jnp.concatenate([vst[hk * dh:(hk + 1) * dh], ones_rows.astype(BF16)], axis=0)
        for qb in range(nq):
            vwt_ref[0, hk, qb] = vwt[hk * dh:(hk + 1) * dh, qb * Q_BLOCK:(qb + 1) * Q_BLOCK]

    zg = _dot(h, w_ref[:, NSA_MAIN:NSA_MAIN + GDN_MAIN])
    qkv_ref[...] = zg[:, 0:3 * GDN_WIDTH]
    zg_ref[...] = zg[:, 3 * GDN_WIDTH:]

    zs = _dot(h, w_ref[:, NSA_MAIN + GDN_MAIN:])
    is_raw = (lane >= SMALL_A) & (lane < SMALL_B)
    small = jnp.where(is_raw, zs, _sigmoid(zs))
    small_ref[...] = small
    small_t = jnp.transpose(small)
    for qb in range(nq):
        gt_ref[0, qb] = small_t[0:SMALL_GATE, qb * Q_BLOCK:(qb + 1) * Q_BLOCK]


def _inproj(x, nw, w_packed, cos2, sin2, layer, *, tm, b, seq):
    t, d = x.shape
    nseq = seq // tm
    nq = tm // Q_BLOCK
    nqb = seq // Q_BLOCK
    hkv, dh = NSA_KV_HEADS, NSA_HEAD_DIM
    cols = NSA_GROUP * Q_BLOCK
    row = lambda i: (i, 0)
    tile5 = lambda i: (i // nseq, 0, i % nseq, 0, 0)
    tile4 = lambda i: (i // nseq, 0, i % nseq, 0)
    return pl.pallas_call(
        functools.partial(_inproj_body, nseq=nseq),
        grid=(t // tm,),
        in_specs=[
            pl.BlockSpec((tm, d), row),
            _layer_spec((1, d), layer, lambda i: (0, 0)),
            _layer_spec((d, W_IN_PACKED), layer, lambda i: (0, 0)),
            pl.BlockSpec((tm, LANES), lambda i: (i % nseq, 0)),
            pl.BlockSpec((tm, LANES), lambda i: (i % nseq, 0)),
        ],
        out_specs=[
            pl.BlockSpec((1, hkv, nq, LANES, cols), tile5),
            pl.BlockSpec((1, hkv, tm, LANES), tile4),
            pl.BlockSpec((1, hkv, 1, V_ROWS, tm), tile5),
            pl.BlockSpec((1, hkv, tm, LANES), tile4),
            pl.BlockSpec((1, hkv, nq, dh, Q_BLOCK), tile5),
            pl.BlockSpec((1, nq, SMALL_GATE, Q_BLOCK), lambda i: (i // nseq, i % nseq, 0, 0)),
            pl.BlockSpec((tm, NSA_KV_WIDTH), row),
            pl.BlockSpec((tm, NSA_KV_WIDTH), row),
            pl.BlockSpec((tm, 3 * GDN_WIDTH), row),
            pl.BlockSpec((tm, GDN_WIDTH), row),
            pl.BlockSpec((tm, LANES), row),
        ],
        out_shape=[
            jax.ShapeDtypeStruct((b, hkv, nqb, LANES, cols), BF16),
            jax.ShapeDtypeStruct((b, hkv, seq, LANES), BF16),
            jax.ShapeDtypeStruct((b, hkv, nseq, V_ROWS, tm), BF16),
            jax.ShapeDtypeStruct((b, hkv, seq, LANES), BF16),
            jax.ShapeDtypeStruct((b, hkv, nqb, dh, Q_BLOCK), BF16),
            jax.ShapeDtypeStruct((b, nqb, SMALL_GATE, Q_BLOCK), F32),
            jax.ShapeDtypeStruct((t, NSA_KV_WIDTH), F32),
            jax.ShapeDtypeStruct((t, NSA_KV_WIDTH), F32),
            jax.ShapeDtypeStruct((t, 3 * GDN_WIDTH), F32),
            jax.ShapeDtypeStruct((t, GDN_WIDTH), F32),
            jax.ShapeDtypeStruct((t, LANES), F32),
        ],
        compiler_params=_cparams(("parallel",)),
        name="inproj",
    )(x, nw, w_packed, cos2, sin2)


def _compress_body(xk_ref, xv_ref, pe_ref, w1_ref, w2_ref, ok_ref, ov_ref):
    nh = ok_ref.shape[2]
    hkv = NSA_KV_HEADS
    lo = [[jnp.zeros((nh, CMP_HIDDEN), F32) for _ in range(hkv)] for _ in range(2)]
    hi = [[jnp.zeros((nh, CMP_HIDDEN), F32) for _ in range(hkv)] for _ in range(2)]
    for l in range(CMP_STRIDE):
        for kv, x_ref in enumerate((xk_ref, xv_ref)):
            xg = x_ref[0, pl.ds(l, nh, stride=CMP_STRIDE), :]
            x_lo = (xg + pe_ref[kv, l:l + 1, :]).astype(BF16)
            x_hi = (xg + pe_ref[kv, CMP_STRIDE + l:CMP_STRIDE + l + 1, :]).astype(BF16)
            for hk in range(hkv):
                lo[kv][hk] = lo[kv][hk] + _dot(x_lo, w1_ref[kv, hk, l])
                hi[kv][hk] = hi[kv][hk] + _dot(x_hi, w1_ref[kv, hk, CMP_STRIDE + l])
    for hk in range(hkv):
        hid_k = lo[0][hk] + pltpu.roll(hi[0][hk], nh - 1, 0)
        hid_v = lo[1][hk] + pltpu.roll(hi[1][hk], nh - 1, 0)
        ok_ref[0, hk] = _dot(_silu(hid_k).astype(BF16), w2_ref[0]).astype(BF16)
        vt = jnp.transpose(_dot(_silu(hid_v).astype(BF16), w2_ref[1]))
        ov_ref[0, hk] = vt[0:NSA_HEAD_DIM].astype(BF16)


def _compress(xk, xv, pe, w1, w2, layer):
    b, seq, wide = xk.shape
    nh = seq // CMP_STRIDE
    hkv = NSA_KV_HEADS
    return pl.pallas_call(
        _compress_body,
        grid=(b,),
        in_specs=[pl.BlockSpec((1, seq, wide), lambda i: (i, 0, 0)),
                  pl.BlockSpec((1, seq, wide), lambda i: (i, 0, 0)),
                  _layer_spec(pe.shape[1:], layer, lambda i: (0, 0, 0)),
                  _layer_spec(w1.shape[1:], layer, lambda i: (0, 0, 0, 0, 0)),
                  _layer_spec(w2.shape[1:], layer, lambda i: (0, 0, 0))],
        out_specs=[pl.BlockSpec((1, hkv, nh, LANES), lambda i: (i, 0, 0, 0)),
                   pl.BlockSpec((1, hkv, NSA_HEAD_DIM, nh), lambda i: (i, 0, 0, 0))],
        out_shape=[jax.ShapeDtypeStruct((b, hkv, nh, LANES), BF16),
                   jax.ShapeDtypeStruct((b, hkv, NSA_HEAD_DIM, nh), BF16)],
        compiler_params=_cparams(("parallel",)),
        name="compress",
    )(xk, xv, pe, w1, w2)


def _colmax(s):
    r = s.shape[0]
    while r > 8 and r % 4 == 0:
        r //= 4
        s = jnp.max(s.reshape(4, r, s.shape[-1]), axis=0)
    return jnp.max(s, axis=0, keepdims=True)


def _nsa_body(q_ref, kc_ref, vc_ref, kaug_ref, vs_ref, kw_ref, vw_ref, g_ref, ovt_ref, cband_ref,
              wband_ref, o_ref, s_ref, b_ref, *, n_slc, n_sel, tk):
    i = pl.program_id(2)
    t0 = i * Q_BLOCK
    cols = NSA_GROUP * Q_BLOCK
    dh = NSA_HEAD_DIM
    q_t = q_ref[0, 0, 0]
    lane = lax.broadcasted_iota(jnp.int32, (1, cols), 1)
    tq = t0 + (lane & (Q_BLOCK - 1))

    def softmax_cols(s):
        e = jnp.exp2(s - _colmax(s))
        return e, jnp.sum(e, axis=0, keepdims=True)

    kc = kc_ref[0, 0]
    nc = kc.shape[0]
    per_q = Q_BLOCK // CMP_STRIDE
    cband = cband_ref[pl.ds(pl.multiple_of(nc - per_q * i, per_q), nc), :]
    e_c, l_c = softmax_cols(_dot(kc, q_t) + cband)
    inv_c = jnp.where(tq >= CMP_BLOCK - 1, 1.0 / l_c, 0.0)
    o_cmp = _dot(vc_ref[0, 0], e_c.astype(BF16)) * inv_c

    p_c = e_c * inv_c
    pg = p_c[:, 0:Q_BLOCK]
    for g in range(1, NSA_GROUP):
        pg = pg + p_c[:, g * Q_BLOCK:(g + 1) * Q_BLOCK]
    imp = _dot_exact_lhs(ovt_ref[...], pg)

    blk = lax.broadcasted_iota(jnp.int32, (LANES, Q_BLOCK), 0)
    tcol = t0 + lax.broadcasted_iota(jnp.int32, (LANES, Q_BLOCK), 1)
    cur = jnp.right_shift(tcol, SLC_BLOCK.bit_length() - 1)
    forced = (blk == 0) | (blk == cur) | (blk == cur - 1)
    valid = blk * SLC_BLOCK <= tcol
    score = jnp.where(forced, FORCE_SCORE, jnp.where(valid, imp, -1.0))
    score = jnp.where(blk < n_slc, score, -jnp.inf)
    bias = jnp.full((LANES, Q_BLOCK), SEL_BIAS, F32)
    for _ in range(n_sel):
        mx = jnp.max(score, axis=0, keepdims=True)
        idx = jnp.min(jnp.where(score == mx, blk, 2 * LANES), axis=0, keepdims=True)
        hit = blk == idx
        bias = jnp.where(hit, 0.0, bias)
        score = jnp.where(hit, -jnp.inf, score)
    b_ref[...] = jnp.concatenate([bias] * NSA_GROUP, axis=1)

    nb = tk // SLC_BLOCK
    pad = jnp.zeros((16 - nb, cols), F32)

    def scores(j):
        rows = jnp.concatenate([b_ref[pl.ds(pl.multiple_of(j * nb, nb), nb), :], pad], axis=0)
        q_aug = jnp.concatenate([q_t[0:dh], rows.astype(BF16), q_t[dh + 16:]], axis=0)
        return _dot(kaug_ref[0, 0, pl.ds(pl.multiple_of(j * tk, tk), tk), :], q_aug)

    def absorb(j, slot, m, acc, causal):
        s = s_ref[slot]
        if causal:
            kpos = j * tk + lax.broadcasted_iota(jnp.int32, (tk, 1), 0)
            s = jnp.where(kpos <= tq, s, NEG_INF)
        m_new = jnp.maximum(m, _colmax(s))
        p = jnp.exp2(s - m_new).astype(BF16)
        return m_new, jnp.exp2(m - m_new) * acc + _dot(vs_ref[0, 0, j], p)

    def slc_pair(jj, carry):
        m, acc = carry
        j = 2 * jj
        s_ref[1] = scores(j + 1)
        m, acc = absorb(j, 0, m, acc, False)
        s_ref[0] = scores(j + 2)
        return absorb(j + 1, 1, m, acc, False)

    def tail_odd(m, acc):
        s_ref[1] = scores(n_full)
        m, acc = absorb(n_full - 1, 0, m, acc, False)
        return absorb(n_full, 1, m, acc, True)

    def tail_even(m, acc):
        return absorb(n_full, 0, m, acc, True)

    n_full = t0 // tk
    s_ref[0] = scores(0)
    init = (jnp.full((1, cols), M_INIT, F32), jnp.zeros((V_ROWS, cols), F32))
    m_s, acc_s = lax.fori_loop(0, n_full // 2, slc_pair, init)
    _, acc_s = lax.cond(n_full % 2 == 1, tail_odd, tail_even, m_s, acc_s)
    o_slc = acc_s[0:dh] * (1.0 / acc_s[dh:dh + 1])

    wspan = Q_BLOCK + WINDOW
    c0 = jnp.maximum(i - WINDOW // Q_BLOCK, 0)
    start = pl.multiple_of(c0 * Q_BLOCK, Q_BLOCK)
    shift = pl.multiple_of(jnp.maximum(WINDOW - t0, 0), Q_BLOCK)
    wband = wband_ref[pl.ds(shift, wspan), :]
    e_w, l_w = softmax_cols(_dot(kw_ref[0, 0, pl.ds(start, wspan), :], q_t) + wband)
    inv_w = 1.0 / l_w
    e_w = e_w.astype(BF16)
    o_win = _dot(vw_ref[0, 0, c0], e_w[0:Q_BLOCK])
    for c in range(1, wspan // Q_BLOCK):
        o_win = o_win + _dot(vw_ref[0, 0, c0 + c], e_w[c * Q_BLOCK:(c + 1) * Q_BLOCK])
    o_win = o_win * inv_w

    gate = g_ref[0, 0]
    outs = []
    for g in range(NSA_GROUP):
        sl = slice(g * Q_BLOCK, (g + 1) * Q_BLOCK)
        outs.append(gate[3 * g:3 * g + 1] * o_cmp[:, sl] + gate[3 * g + 1:3 * g + 2] * o_slc[:, sl]
                    + gate[3 * g + 2:3 * g + 3] * o_win[:, sl])
    pairs = [jnp.transpose(jnp.concatenate(outs[2 * k:2 * k + 2], axis=0)) for k in range(NSA_GROUP // 2)]
    o_ref[0] = jnp.concatenate(pairs, axis=1).astype(BF16)


def _nsa(q_t, kcmp, vcmp_t, kaug, vs_t, kw, vw_t, gates_t, ovt, cband, wband):
    b, hkv, nqb, _, cols = q_t.shape
    seq = kaug.shape[2]
    tk = vs_t.shape[4]
    n_slc = seq // SLC_BLOCK
    per_head = lambda a: pl.BlockSpec((1, 1) + a.shape[2:],
                                      lambda bi, hi, qi: (bi, hi) + (0,) * (a.ndim - 2))
    const = lambda a: pl.BlockSpec(a.shape, lambda bi, hi, qi: (0, 0))
    body = functools.partial(_nsa_body, n_slc=n_slc, n_sel=min(N_SELECTED, n_slc), tk=tk)
    return pl.pallas_call(
        body,
        grid=(b, hkv, nqb),
        in_specs=[pl.BlockSpec((1, 1, 1) + q_t.shape[3:], lambda bi, hi, qi: (bi, hi, qi, 0, 0)),
                  per_head(kcmp), per_head(vcmp_t), per_head(kaug), per_head(vs_t),
                  per_head(kw), per_head(vw_t),
                  pl.BlockSpec((1, 1, GATE_ROWS, Q_BLOCK), lambda bi, hi, qi: (bi, qi, hi, 0)),
                  const(ovt), const(cband), const(wband)],
        out_specs=pl.BlockSpec((1, Q_BLOCK, NSA_GROUP * NSA_HEAD_DIM), lambda bi, hi, qi: (bi, qi, hi)),
        out_shape=jax.ShapeDtypeStruct((b, seq, NSA_WIDTH), BF16),
        scratch_shapes=[pltpu.VMEM((2, tk, cols), F32), pltpu.VMEM((LANES, cols), F32)],
        compiler_params=_cparams(("parallel", "parallel", "arbitrary")),
        name="nsa",
    )(q_t, kcmp, vcmp_t, kaug, vs_t, kw, vw_t, gates_t, ovt, cband, wband)


GDN_TILE = 2 * GDN_CHUNK


def _gdn_prep_body(qkv_ref, halo_ref, cw_ref, small_ref, alog_ref, dtb_ref,
                   u_ref, wq_ref, kdt_ref, attn_ref, eg_ref, *, ts):
    i = pl.program_id(1)
    x = qkv_ref[0]
    halo = jnp.where(i > 0, halo_ref[0], 0.0)
    xx = jnp.concatenate([halo, x], axis=0)
    y = x * cw_ref[CONV_WIDTH - 1:CONV_WIDTH, :]
    for d in range(1, CONV_WIDTH):
        shifted = pltpu.roll(xx, d, 0)[8:]
        y = y + shifted * cw_ref[CONV_WIDTH - 1 - d:CONV_WIDTH - d, :]
    y = _silu(y)

    sm = small_ref[0]
    sp_in = sm + dtb_ref[...]
    softplus = jnp.maximum(sp_in, 0.0) + jnp.log(1.0 + jnp.exp(-jnp.abs(sp_in)))
    glog = -jnp.exp(alog_ref[...]) * softplus

    ri = lax.broadcasted_iota(jnp.int32, (ts, ts), 0)
    ci = lax.broadcasted_iota(jnp.int32, (ts, ts), 1)
    sh = GDN_CHUNK.bit_length() - 1
    same = jnp.right_shift(ri, sh) == jnp.right_shift(ci, sh)
    tril = jnp.where(same & (ri >= ci), 1.0, 0.0).astype(BF16)
    ones = jnp.where(same, 1.0, 0.0).astype(BF16)
    gcum = _dot_exact_lhs(tril, glog)
    glast = _dot_exact_lhs(ones, glog)

    r2 = lax.broadcasted_iota(jnp.int32, (GDN_TILE, GDN_TILE), 0)
    c2 = lax.broadcasted_iota(jnp.int32, (GDN_TILE, GDN_TILE), 1)
    same2 = jnp.right_shift(r2, sh) == jnp.right_shift(c2, sh)
    incl = same2 & (r2 >= c2)
    strict = same2 & (r2 > c2)
    eye = jnp.where(r2 == c2, 1.0, 0.0)
    qscale = GDN_HEAD_DIM ** -0.5

    units = []
    for c in range(ts // GDN_TILE):
        r0 = c * GDN_TILE
        gc_tile = gcum[r0:r0 + GDN_TILE]
        gc_rows = jnp.transpose(gc_tile)
        for h in range(GDN_HEADS):
            lo = h * GDN_HEAD_DIM
            qh = y[r0:r0 + GDN_TILE, lo:lo + GDN_HEAD_DIM]
            kh = y[r0:r0 + GDN_TILE, GDN_WIDTH + lo:GDN_WIDTH + lo + GDN_HEAD_DIM]
            vh = y[r0:r0 + GDN_TILE, 2 * GDN_WIDTH + lo:2 * GDN_WIDTH + lo + GDN_HEAD_DIM]
            qh = qh * lax.rsqrt(jnp.sum(qh * qh, axis=-1, keepdims=True) + EPS)
            kh = kh * lax.rsqrt(jnp.sum(kh * kh, axis=-1, keepdims=True) + EPS)
            gc_col = gc_tile[:, SMALL_A + h:SMALL_A + h + 1]
            gc_row = gc_rows[SMALL_A + h:SMALL_A + h + 1, :]
            gl_col = glast[r0:r0 + GDN_TILE, SMALL_A + h:SMALL_A + h + 1]
            beta = sm[r0:r0 + GDN_TILE, SMALL_B + h:SMALL_B + h + 1]

            decay = jnp.where(incl, jnp.exp(jnp.minimum(gc_col - gc_row, 0.0)), 0.0)
            kb = kh * beta
            k_bf = kh.astype(BF16)
            a_s = jnp.where(strict, _dot_nt(kb.astype(BF16), k_bf) * decay, 0.0)
            egc = jnp.exp(gc_col)
            qs = qh * qscale
            attn = jnp.where(incl, _dot_nt(qs.astype(BF16), k_bf) * decay, 0.0)
            attn_ref[0, h, r0:r0 + GDN_TILE, :] = attn.astype(BF16)
            k_dec = kh * jnp.exp(gl_col - gc_col)
            kdt_ref[0, h, r0:r0 + GDN_TILE, :] = jnp.transpose(k_dec).astype(BF16)
            for cc in range(2):
                e0 = (r0 // GDN_CHUNK + cc) * 8
                eg_ref[0, h, e0:e0 + 8, :] = jnp.broadcast_to(
                    jnp.exp(gl_col[cc * GDN_CHUNK:cc * GDN_CHUNK + 8]), (8, GDN_HEAD_DIM))
            units.append(dict(h=h, r0=r0, a=a_s, rhs_u=vh * beta, rhs_w=kb * egc, q_dec=qs * egc))

    xinv = [eye - un['a'] for un in units]
    asp = [_split2(un['a']) for un in units]
    pw = [_dot_split(a, a) for a in asp]
    steps = GDN_CHUNK.bit_length() - 2
    for s in range(steps):
        pws = [_split2(p) for p in pw]
        xinv = [x + _dot_split(_split2(x), p) for x, p in zip(xinv, pws)]
        if s + 1 < steps:
            pw = [_dot_split(p, p) for p in pws]

    for un, x in zip(units, xinv):
        h, r0 = un['h'], un['r0']
        xs = _split2(x)
        u = _dot_split(xs, _split2(un['rhs_u']))
        w = _dot_split(xs, _split2(un['rhs_w']))
        u_ref[0, h, r0:r0 + GDN_TILE, :] = u
        for cc in range(2):
            a0 = cc * GDN_CHUNK
            wq = jnp.concatenate([w[a0:a0 + GDN_CHUNK], un['q_dec'][a0:a0 + GDN_CHUNK]], axis=0)
            n0 = 2 * r0 + cc * GDN_TILE
            wq_ref[0, h, n0:n0 + GDN_TILE, :] = wq.astype(BF16)


def _gdn_prep(qkv, cw, small, alog_row, dtb_row, layer, *, ts):
    b, seq, wide = qkv.shape
    nt = seq // ts
    hd = GDN_HEAD_DIM
    hspec = lambda rows: pl.BlockSpec((1, GDN_HEADS, rows, hd), lambda bi, ti: (bi, 0, ti, 0))
    hshape = lambda rows, dt: jax.ShapeDtypeStruct((b, GDN_HEADS, rows, hd), dt)
    return pl.pallas_call(
        functools.partial(_gdn_prep_body, ts=ts),
        grid=(b, nt),
        in_specs=[
            pl.BlockSpec((1, ts, wide), lambda bi, ti: (bi, ti, 0)),
            pl.BlockSpec((1, 8, wide), lambda bi, ti: (bi, jnp.maximum(ti * (ts // 8) - 1, 0), 0)),
            _layer_spec(cw.shape[1:], layer, lambda bi, ti: (0, 0)),
            pl.BlockSpec((1, ts, LANES), lambda bi, ti: (bi, ti, 0)),
            _layer_spec((1, LANES), layer, lambda bi, ti: (0, 0)),
            _layer_spec((1, LANES), layer, lambda bi, ti: (0, 0)),
        ],
        out_specs=[hspec(ts), hspec(2 * ts), hspec(ts), hspec(ts), hspec(ts // 8)],
        out_shape=[hshape(seq, F32), hshape(2 * seq, BF16), hshape(seq, BF16),
                   hshape(seq, BF16), hshape(seq // 8, F32)],
        compiler_params=_cparams(("parallel", "parallel")),
        name="gdn_prep",
    )(qkv, qkv, cw, small, alog_row, dtb_row)


def _gdn_scan_body(u_ref, wq_ref, kdt_ref, attn_ref, eg_ref, zg_ref, gn_ref, o_ref, st_ref, *, ts):
    @pl.when(pl.program_id(1) == 0)
    def _():
        st_ref[...] = jnp.zeros_like(st_ref)

    hd = GDN_HEAD_DIM
    gn = gn_ref[...]
    zeros = jnp.zeros((GDN_CHUNK, hd), F32)
    states = [st_ref[h] for h in range(GDN_HEADS)]
    for n in range(ts // GDN_CHUNK):
        r0 = n * GDN_CHUNK
        t0 = (n // 2) * GDN_TILE
        for h in range(GDN_HEADS):
            state = states[h]
            r = _dot(wq_ref[0, h, 2 * r0:2 * r0 + GDN_TILE, :], state.astype(BF16))
            v_new = u_ref[0, h, r0:r0 + GDN_CHUNK, :] - r[0:GDN_CHUNK]
            vpad = jnp.concatenate([v_new, zeros] if n % 2 == 0 else [zeros, v_new], axis=0)
            vpad = vpad.astype(BF16)
            o = r[GDN_CHUNK:] + _dot(attn_ref[0, h, r0:r0 + GDN_CHUNK, :], vpad)
            states[h] = state * eg_ref[0, h, 8 * n:8 * n + 1, :] + _dot(kdt_ref[0, h, t0:t0 + GDN_TILE, :], vpad)
            on = o * lax.rsqrt(jnp.mean(o * o, axis=-1, keepdims=True) + EPS) * gn
            gate = _silu(zg_ref[0, r0:r0 + GDN_CHUNK, h * hd:(h + 1) * hd])
            o_ref[0, r0:r0 + GDN_CHUNK, h * hd:(h + 1) * hd] = (on * gate).astype(BF16)
    for h in range(GDN_HEADS):
        st_ref[h] = states[h]


def _gdn_scan(u, wq, kdt, attn, eg, zg, gn, layer, *, ts):
    b, nh, seq, hd = u.shape
    hspec = lambda rows: pl.BlockSpec((1, nh, rows, hd), lambda bi, ti: (bi, 0, ti, 0))
    return pl.pallas_call(
        functools.partial(_gdn_scan_body, ts=ts),
        grid=(b, seq // ts),
        in_specs=[hspec(ts), hspec(2 * ts), hspec(ts), hspec(ts), hspec(ts // 8),
                  pl.BlockSpec((1, ts, nh * hd), lambda bi, ti: (bi, ti, 0)),
                  _layer_spec((1, hd), layer, lambda bi, ti: (0, 0))],
        out_specs=pl.BlockSpec((1, ts, nh * hd), lambda bi, ti: (bi, ti, 0)),
        out_shape=jax.ShapeDtypeStruct((b, seq, nh * hd), BF16),
        scratch_shapes=[pltpu.VMEM((nh, hd, hd), F32)],
        compiler_params=_cparams(("parallel", "arbitrary")),
        name="gdn_scan",
    )(u, wq, kdt, attn, eg, zg, gn)


def _outproj_body(x_ref, a_ref, b_ref, w_ref, o_ref):
    half = a_ref.shape[1]
    o_ref[...] = x_ref[...] + _dot(a_ref[...], w_ref[0:half, :]) + _dot(b_ref[...], w_ref[half:, :])


def _outproj(x, o_nsa, o_gdn, w_out, layer, *, tm):
    t, d = x.shape
    row = lambda i: (i, 0)
    return pl.pallas_call(
        _outproj_body,
        grid=(t // tm,),
        in_specs=[pl.BlockSpec((tm, d), row), pl.BlockSpec((tm, o_nsa.shape[1]), row),
                  pl.BlockSpec((tm, o_gdn.shape[1]), row),
                  _layer_spec(w_out.shape[1:], layer, lambda i: (0, 0))],
        out_specs=pl.BlockSpec((tm, d), row),
        out_shape=jax.ShapeDtypeStruct((t, d), F32),
        compiler_params=_cparams(("parallel",)),
        name="outproj",
    )(x, o_nsa, o_gdn, w_out)


def _ple_body(x_ref, p_ref, nw_ref, wg_ref, wp_ref, fn_ref, o_ref, *, final):
    x = x_ref[...]
    h = _rms(x, nw_ref[...]).astype(BF16)
    gate = _sigmoid(_dot(h, wg_ref[...]))
    out = x + gate * _dot(p_ref[...].astype(BF16), wp_ref[...])
    if final:
        out = _rms(out, fn_ref[...])
    o_ref[...] = out


def _ple(x, p, nw, wg, wp, fn, layer, *, tm, final):
    t, d = x.shape
    row = lambda i: (i, 0)
    return pl.pallas_call(
        functools.partial(_ple_body, final=final),
        grid=(t // tm,),
        in_specs=[pl.BlockSpec((tm, d), row), _layer_spec((tm, p.shape[2]), layer, row),
                  _layer_spec((1, d), layer, lambda i: (0, 0)),
                  _layer_spec(wg.shape[1:], layer, lambda i: (0, 0)),
                  _layer_spec(wp.shape[1:], layer, lambda i: (0, 0)),
                  pl.BlockSpec(fn.shape, lambda i: (0, 0))],
        out_specs=pl.BlockSpec((tm, d), row),
        out_shape=jax.ShapeDtypeStruct((t, d), F32),
        compiler_params=_cparams(("parallel",)),
        name="ple",
    )(x, p, nw, wg, wp, fn)


def _pack_w_in(w_in):
    offs = [0]
    for s in IN_SIZES:
        offs.append(offs[-1] + s)
    main = w_in[..., offs[0]:offs[7]]
    gates = w_in[..., offs[7]:offs[8]]
    gdn = w_in[..., offs[8]:offs[10]]
    ab = w_in[..., offs[10]:offs[12]]
    per_head = 3 * NSA_GROUP
    zpad = lambda n: jnp.zeros(w_in.shape[:-1] + (n,), w_in.dtype)
    gate_cols = []
    for hk in range(NSA_KV_HEADS):
        gate_cols += [gates[..., hk * per_head:(hk + 1) * per_head], zpad(GATE_ROWS - per_head)]
    tail = zpad(LANES - SMALL_GATE - ab.shape[-1])
    return jnp.concatenate([main, gdn] + gate_cols + [ab, tail], axis=-1).astype(BF16)


def _pack_cmp_w1(w1):
    depth = w1.shape[0]
    w1r = w1.reshape(depth, CMP_BLOCK, NSA_HEAD_DIM, CMP_HIDDEN).astype(BF16)
    z = jnp.zeros_like(w1r)
    return jnp.stack([jnp.concatenate([w1r, z], axis=2), jnp.concatenate([z, w1r], axis=2)], axis=1)


def _rope_tables(seq):
    dim = NSA_HEAD_DIM
    inv = 1.0 / (ROPE_THETA ** (jnp.arange(0, dim, 2, dtype=F32) / dim))
    ang = jnp.arange(seq, dtype=F32)[:, None] * inv[None, :]
    ang = jnp.concatenate([ang, ang], axis=-1)
    cos, sin = jnp.cos(ang), jnp.sin(ang)
    sign = jnp.where(jnp.arange(dim) < dim // 2, -1.0, 1.0).astype(F32)
    return jnp.tile(cos, (1, LANES // dim)), jnp.tile(sin * sign[None, :], (1, LANES // dim))


def _overlap_t(seq, nc_pad):
    n_slc = seq // SLC_BLOCK
    jc = jnp.arange(nc_pad)[None, :]
    js = jnp.arange(LANES)[:, None]
    ov = ((jc * CMP_STRIDE < (js + 1) * SLC_BLOCK) & (jc * CMP_STRIDE + CMP_BLOCK > js * SLC_BLOCK)
          & (js < n_slc))
    return ov.astype(BF16)


def _mask_bands(seq):
    q = (jnp.arange(NSA_GROUP * Q_BLOCK) % Q_BLOCK)[None, :]
    nc = seq // CMP_STRIDE
    rel = jnp.arange(2 * nc)[:, None] - nc
    cband = jnp.where(rel * CMP_STRIDE + (CMP_BLOCK - 1) <= q, 0.0, NEG_INF).astype(F32)
    r = jnp.arange(2 * WINDOW + Q_BLOCK)[:, None]
    wband = jnp.where((q < r) & (r <= q + WINDOW), 0.0, NEG_INF).astype(F32)
    return cband, wband


def _layer(x2, layer, w, consts, *, b, seq, final, cfg):
    t = b * seq
    x2 = _ffn(x2, w['ffn1_norm'], w['ffn1_w1'], w['ffn1_w3'], w['ffn1_w2'], layer,
              tm=cfg['ffn_tm'], tf=cfg['ffn_tf'])
    (q_t, kaug, vs_t, kw, vw_t, gates_t, cmpk, cmpv, qkv, zg, small) = _inproj(
        x2, w['mix_norm'], w['w_in'], consts['cos'], consts['sin'], layer, tm=cfg['in_tm'], b=b, seq=seq)
    kcmp, vcmp_t = _compress(cmpk.reshape(b, seq, LANES), cmpv.reshape(b, seq, LANES), w['cmp_pe'],
                             w['cmp_w1'], w['cmp_w2'], layer)
    o_nsa = _nsa(q_t, kcmp, vcmp_t, kaug, vs_t, kw, vw_t, gates_t, consts['ovt'], consts['cband'],
                 consts['wband'])
    u, wq, kdt, attn, eg = _gdn_prep(qkv.reshape(b, seq, 3 * GDN_WIDTH), w['gdn_conv'],
                                     small.reshape(b, seq, LANES), w['gdn_a_log'], w['gdn_dt_bias'],
                                     layer, ts=cfg['prep_ts'])
    o_gdn = _gdn_scan(u, wq, kdt, attn, eg, zg.reshape(b, seq, GDN_WIDTH), w['gdn_norm'], layer,
                      ts=cfg['scan_ts'])
    x2 = _outproj(x2, o_nsa.reshape(t, NSA_WIDTH), o_gdn.reshape(t, GDN_WIDTH), w['w_out'], layer,
                  tm=cfg['out_tm'])
    x2 = _ffn(x2, w['ffn2_norm'], w['ffn2_w1'], w['ffn2_w3'], w['ffn2_w2'], layer,
              tm=cfg['ffn_tm'], tf=cfg['ffn_tf'])
    return _ple(x2, w['p'], w['ple_norm'], w['ple_gate'], w['ple_proj'], consts['final_norm'], layer,
                tm=cfg['ple_tm'], final=final)


DEFAULT_CFG = dict(ffn_tm=512, ffn_tf=1408, in_tm=512, prep_ts=256, scan_ts=256,
                   out_tm=512, ple_tm=512)


def _forward(x, p, w, cfg):
    b, seq, d = x.shape
    depth = p.shape[0]
    t = b * seq
    cos2, sin2 = _rope_tables(seq)
    cband, wband = _mask_bands(seq)
    consts = dict(cos=cos2, sin=sin2, ovt=_overlap_t(seq, seq // CMP_STRIDE), cband=cband, wband=wband,
                  final_norm=w['final_norm'].reshape(1, d))
    bf = lambda a: a.astype(BF16)
    row3 = lambda a: a.reshape(depth, 1, a.shape[-1])
    lane_rows = lambda v, off: jnp.zeros((depth, 1, LANES), F32).at[:, 0, off:off + v.shape[1]].set(v)
    pe = jnp.stack([w['cmp_pe_k'], w['cmp_pe_v']], axis=1)
    w2 = jnp.stack([w['cmp_k_w2'], w['cmp_v_w2']], axis=1)
    ws = dict(
        p=p.reshape(depth, t, p.shape[-1]),
        ffn1_norm=row3(w['ffn1_norm']), ffn1_w1=bf(w['ffn1_w1']), ffn1_w3=bf(w['ffn1_w3']),
        ffn1_w2=bf(w['ffn1_w2']),
        mix_norm=row3(w['mix_norm']), w_in=_pack_w_in(w['w_in']),
        cmp_pe=jnp.concatenate([pe, pe], axis=-1),
        cmp_w1=jnp.stack([_pack_cmp_w1(w['cmp_k_w1']), _pack_cmp_w1(w['cmp_v_w1'])], axis=1),
        cmp_w2=bf(jnp.pad(w2, ((0, 0), (0, 0), (0, 0), (0, LANES - w2.shape[-1])))),
        gdn_conv=w['gdn_conv'], gdn_a_log=lane_rows(w['gdn_a_log'], SMALL_A),
        gdn_dt_bias=lane_rows(w['gdn_dt_bias'], SMALL_A), gdn_norm=row3(w['gdn_norm']),
        w_out=bf(w['w_out']),
        ffn2_norm=row3(w['ffn2_norm']), ffn2_w1=bf(w['ffn2_w1']), ffn2_w3=bf(w['ffn2_w3']),
        ffn2_w2=bf(w['ffn2_w2']),
        ple_norm=row3(w['ple_norm']), ple_gate=bf(w['ple_gate']), ple_proj=bf(w['ple_proj']),
    )
    x2 = x.reshape(t, d)
    for i in range(depth):
        x2 = _layer(x2, i, ws, consts, b=b, seq=seq, final=(i == depth - 1), cfg=cfg)
    return x2.reshape(b, seq, d)


def kernel(x, p, ffn1_norm, ffn1_w1, ffn1_w3, ffn1_w2, mix_norm, w_in, cmp_pe_k, cmp_pe_v,
           cmp_k_w1, cmp_k_w2, cmp_v_w1, cmp_v_w2, gdn_conv, gdn_a_log, gdn_dt_bias, gdn_norm,
           w_out, ffn2_norm, ffn2_w1, ffn2_w3, ffn2_w2, ple_norm, ple_gate, ple_proj, final_norm):
    w = dict(ffn1_norm=ffn1_norm, ffn1_w1=ffn1_w1, ffn1_w3=ffn1_w3, ffn1_w2=ffn1_w2,
             mix_norm=mix_norm, w_in=w_in, cmp_pe_k=cmp_pe_k, cmp_pe_v=cmp_pe_v,
             cmp_k_w1=cmp_k_w1, cmp_k_w2=cmp_k_w2, cmp_v_w1=cmp_v_w1, cmp_v_w2=cmp_v_w2,
             gdn_conv=gdn_conv, gdn_a_log=gdn_a_log, gdn_dt_bias=gdn_dt_bias, gdn_norm=gdn_norm,
             w_out=w_out, ffn2_norm=ffn2_norm, ffn2_w1=ffn2_w1, ffn2_w3=ffn2_w3, ffn2_w2=ffn2_w2,
             ple_norm=ple_norm, ple_gate=ple_gate, ple_proj=ple_proj, final_norm=final_norm)
    return _forward(x, p, w, DEFAULT_CFG)
```

```python
import functools

import jax
import jax.numpy as jnp
from jax import lax
from jax.experimental import pallas as pl
from jax.experimental.pallas import tpu as pltpu

F32 = jnp.float32
BF16 = jnp.bfloat16

D_MODEL = 1024
NSA_HEADS = 8
NSA_KV_HEADS = 2
NSA_GROUP = NSA_HEADS // NSA_KV_HEADS
NSA_HEAD_DIM = 64
CMP_BLOCK = 32
CMP_STRIDE = 16
CMP_HIDDEN = 128
SLC_BLOCK = 64
N_SELECTED = 16
WINDOW = 512
Q_BLOCK = 128
GDN_HEADS = 4
GDN_HEAD_DIM = 128
GDN_CHUNK = 64
CONV_WIDTH = 4
D_FF = 2816
PLE_DIM = 256
ROPE_THETA = 10000.0
EPS = 1e-6
FORCE_SCORE = 1e6
NEG_INF = -1e30

NSA_WIDTH = NSA_HEADS * NSA_HEAD_DIM
NSA_KV_WIDTH = NSA_KV_HEADS * NSA_HEAD_DIM
GDN_WIDTH = GDN_HEADS * GDN_HEAD_DIM
IN_SIZES = (NSA_WIDTH, NSA_KV_WIDTH, NSA_KV_WIDTH, NSA_KV_WIDTH, NSA_KV_WIDTH,
            NSA_KV_WIDTH, NSA_KV_WIDTH, 3 * NSA_HEADS, 3 * GDN_WIDTH, GDN_WIDTH,
            GDN_HEADS, GDN_HEADS)

LANES = 128
NSA_MAIN = NSA_WIDTH + 6 * NSA_KV_WIDTH
GDN_MAIN = 4 * GDN_WIDTH
GATE_ROWS = 16
SMALL_GATE = NSA_KV_HEADS * GATE_ROWS
SMALL_A = SMALL_GATE
SMALL_B = SMALL_GATE + GDN_HEADS
W_IN_PACKED = NSA_MAIN + GDN_MAIN + LANES
V_ROWS = NSA_HEAD_DIM + 16
SEL_BIAS = -2.0 ** 100
M_INIT = -3.0e38
LOG2_E = 1.4426950408889634
VMEM_LIMIT = 56 * 1024 * 1024


def _cparams(sem):
    return pltpu.CompilerParams(dimension_semantics=sem, vmem_limit_bytes=VMEM_LIMIT)


def _rms(x, w):
    ms = jnp.mean(x * x, axis=-1, keepdims=True)
    return x * lax.rsqrt(ms + EPS) * w


def _sigmoid(x):
    return 1.0 / (1.0 + jnp.exp(-x))


def _silu(x):
    return x * _sigmoid(x)


def _dot(a, b):
    return jnp.dot(a, b, preferred_element_type=F32)


def _dot_nt(a, b):
    return lax.dot_general(a, b, (((1,), (1,)), ((), ())), preferred_element_type=F32)


def _split3(x):
    hi = x.astype(BF16)
    r = x - hi.astype(F32)
    mid = r.astype(BF16)
    lo = (r - mid.astype(F32)).astype(BF16)
    return hi, mid, lo


def _dot_exact_lhs(a_bf, x):
    hi, mid, lo = _split3(x)
    return _dot(a_bf, hi) + (_dot(a_bf, mid) + _dot(a_bf, lo))


def _split2(x):
    hi = x.astype(BF16)
    return hi, (x - hi.astype(F32)).astype(BF16)


def _dot_split(a, b):
    ah, am = a
    bh, bm = b
    return _dot(ah, bh) + (_dot(ah, bm) + _dot(am, bh))


def _layer_spec(block, layer, index):
    return pl.BlockSpec((None,) + block, lambda *g: (layer,) + index(*g))


def _ffn_body(x_ref, nw_ref, w1_ref, w3_ref, w2_ref, o_ref, h_ref, acc_ref):
    j = pl.program_id(1)

    @pl.when(j == 0)
    def _():
        h_ref[...] = _rms(x_ref[...], nw_ref[...]).astype(BF16)
        acc_ref[...] = jnp.zeros_like(acc_ref)

    h = h_ref[...]
    u = _dot(h, w1_ref[...])
    g = _dot(h, w3_ref[...])
    a = (_silu(u) * g).astype(BF16)
    acc_ref[...] += _dot(a, w2_ref[...])

    @pl.when(j == pl.num_programs(1) - 1)
    def _():
        o_ref[...] = x_ref[...] + 0.5 * acc_ref[...]


def _ffn(x, nw, w1, w3, w2, layer, *, tm, tf):
    t, d = x.shape
    ff = w1.shape[2]
    return pl.pallas_call(
        _ffn_body,
        grid=(t // tm, ff // tf),
        in_specs=[
            pl.BlockSpec((tm, d), lambda i, j: (i, 0)),
            _layer_spec((1, d), layer, lambda i, j: (0, 0)),
            _layer_spec((d, tf), layer, lambda i, j: (0, j)),
            _layer_spec((d, tf), layer, lambda i, j: (0, j)),
            _layer_spec((tf, d), layer, lambda i, j: (j, 0)),
        ],
        out_specs=pl.BlockSpec((tm, d), lambda i, j: (i, 0)),
        out_shape=jax.ShapeDtypeStruct((t, d), F32),
        scratch_shapes=[pltpu.VMEM((tm, d), BF16), pltpu.VMEM((tm, d), F32)],
        compiler_params=_cparams(("parallel", "arbitrary")),
        name="ffn",
    )(x, nw, w1, w3, w2)


def _rope(xg, cos, sin_signed, first_half):
    fwd = pltpu.roll(xg, LANES - NSA_HEAD_DIM // 2, 1)
    bwd = pltpu.roll(xg, NSA_HEAD_DIM // 2, 1)
    return xg * cos + jnp.where(first_half, fwd, bwd) * sin_signed


def _inproj_body(x_ref, nw_ref, w_ref, cos_ref, sin_ref,
                 qt_ref, kaug_ref, vst_ref, kw_ref, vwt_ref, gt_ref,
                 cmpk_ref, cmpv_ref, qkv_ref, zg_ref, small_ref, *, nseq):
    tm = x_ref.shape[0]
    nq = tm // Q_BLOCK
    dh = NSA_HEAD_DIM
    h = _rms(x_ref[...], nw_ref[...]).astype(BF16)
    cos = cos_ref[...]
    sin_s = sin_ref[...]
    lane = lax.broadcasted_iota(jnp.int32, (1, LANES), 1)
    first_half = (lane & (dh - 1)) < (dh // 2)
    low = lane < dh

    z = _dot(h, w_ref[:, 0:NSA_MAIN])
    scale = dh ** -0.5 * LOG2_E
    for pair in range(NSA_WIDTH // LANES):
        zq = _rope(z[:, pair * LANES:(pair + 1) * LANES], cos, sin_s, first_half) * scale
        tr = jnp.transpose(zq).astype(BF16)
        for half in range(2):
            hk, g = divmod(2 * pair + half, NSA_GROUP)
            for qb in range(nq):
                qt_ref[0, hk, qb, 0:dh, g * Q_BLOCK:(g + 1) * Q_BLOCK] = (
                    tr[half * dh:(half + 1) * dh, qb * Q_BLOCK:(qb + 1) * Q_BLOCK])
    qt_ref[0, :, :, dh:, :] = jnp.zeros((NSA_KV_HEADS, nq, LANES - dh, NSA_GROUP * Q_BLOCK), BF16)

    def group(c, rotary):
        zc = z[:, NSA_WIDTH + c * LANES: NSA_WIDTH + (c + 1) * LANES]
        return _rope(zc, cos, sin_s, first_half) if rotary else zc

    def heads(zc):
        return [jnp.where(low, zc, 0.0), jnp.where(low, pltpu.roll(zc, dh, 1), 0.0)]

    cmpk_ref[...] = group(0, True)
    cmpv_ref[...] = group(1, False)
    tok = (pl.program_id(0) % nseq) * tm + lax.broadcasted_iota(jnp.int32, (tm, 1), 0)
    local_blk = jnp.right_shift(tok, SLC_BLOCK.bit_length() - 1) & (tm // SLC_BLOCK - 1)
    onehot = jnp.where(lane - dh == local_blk, 1.0, 0.0)
    for hk, kh in enumerate(heads(group(2, True))):
        kaug_ref[0, hk] = jnp.where(low, kh, onehot).astype(BF16)
    for hk, kh in enumerate(heads(group(4, True))):
        kw_ref[0, hk] = kh.astype(BF16)
    vst = jnp.transpose(group(3, False)).astype(BF16)
    ones_rows = jnp.where(lax.broadcasted_iota(jnp.int32, (V_ROWS - dh, tm), 0) == 0, 1.0, 0.0)
    vwt = jnp.transpose(group(5, False)).astype(BF16)
    for hk in range(NSA_KV_HEADS):
        vst_ref[0, hk, 0] = jnp.concatenate([vst[hk * dh:(hk + 1) * dh], ones_rows.astype(BF16)], axis=0)
        for qb in range(nq):
            vwt_ref[0, hk, qb] = vwt[hk * dh:(hk + 1) * dh, qb * Q_BLOCK:(qb + 1) * Q_BLOCK]

    zg = _dot(h, w_ref[:, NSA_MAIN:NSA_MAIN + GDN_MAIN])
    qkv_ref[...] = zg[:, 0:3 * GDN_WIDTH]
    zg_ref[...] = zg[:, 3 * GDN_WIDTH:]

    zs = _dot(h, w_ref[:, NSA_MAIN + GDN_MAIN:])
    is_raw = (lane >= SMALL_A) & (lane < SMALL_B)
    small = jnp.where(is_raw, zs, _sigmoid(zs))
    small_ref[...] = small
    small_t = jnp.transpose(small)
    for qb in range(nq):
        gt_ref[0, qb] = small_t[0:SMALL_GATE, qb * Q_BLOCK:(qb + 1) * Q_BLOCK]


def _inproj(x, nw, w_packed, cos2, sin2, layer, *, tm, b, seq):
    t, d = x.shape
    nseq = seq // tm
    nq = tm // Q_BLOCK
    nqb = seq // Q_BLOCK
    hkv, dh = NSA_KV_HEADS, NSA_HEAD_DIM
    cols = NSA_GROUP * Q_BLOCK
    row = lambda i: (i, 0)
    tile5 = lambda i: (i // nseq, 0, i % nseq, 0, 0)
    tile4 = lambda i: (i // nseq, 0, i % nseq, 0)
    return pl.pallas_call(
        functools.partial(_inproj_body, nseq=nseq),
        grid=(t // tm,),
        in_specs=[
            pl.BlockSpec((tm, d), row),
            _layer_spec((1, d), layer, lambda i: (0, 0)),
            _layer_spec((d, W_IN_PACKED), layer, lambda i: (0, 0)),
            pl.BlockSpec((tm, LANES), lambda i: (i % nseq, 0)),
            pl.BlockSpec((tm, LANES), lambda i: (i % nseq, 0)),
        ],
        out_specs=[
            pl.BlockSpec((1, hkv, nq, LANES, cols), tile5),
            pl.BlockSpec((1, hkv, tm, LANES), tile4),
            pl.BlockSpec((1, hkv, 1, V_ROWS, tm), tile5),
            pl.BlockSpec((1, hkv, tm, LANES), tile4),
            pl.BlockSpec((1, hkv, nq, dh, Q_BLOCK), tile5),
            pl.BlockSpec((1, nq, SMALL_GATE, Q_BLOCK), lambda i: (i // nseq, i % nseq, 0, 0)),
            pl.BlockSpec((tm, NSA_KV_WIDTH), row),
            pl.BlockSpec((tm, NSA_KV_WIDTH), row),
            pl.BlockSpec((tm, 3 * GDN_WIDTH), row),
            pl.BlockSpec((tm, GDN_WIDTH), row),
            pl.BlockSpec((tm, LANES), row),
        ],
        out_shape=[
            jax.ShapeDtypeStruct((b, hkv, nqb, LANES, cols), BF16),
            jax.ShapeDtypeStruct((b, hkv, seq, LANES), BF16),
            jax.ShapeDtypeStruct((b, hkv, nseq, V_ROWS, tm), BF16),
            jax.ShapeDtypeStruct((b, hkv, seq, LANES), BF16),
            jax.ShapeDtypeStruct((b, hkv, nqb, dh, Q_BLOCK), BF16),
            jax.ShapeDtypeStruct((b, nqb, SMALL_GATE, Q_BLOCK), F32),
            jax.ShapeDtypeStruct((t, NSA_KV_WIDTH), F32),
            jax.ShapeDtypeStruct((t, NSA_KV_WIDTH), F32),
            jax.ShapeDtypeStruct((t, 3 * GDN_WIDTH), F32),
            jax.ShapeDtypeStruct((t, GDN_WIDTH), F32),
            jax.ShapeDtypeStruct((t, LANES), F32),
        ],
        compiler_params=_cparams(("parallel",)),
        name="inproj",
    )(x, nw, w_packed, cos2, sin2)


def _compress_body(xk_ref, xv_ref, pe_ref, w1_ref, w2_ref, ok_ref, ov_ref):
    nh = ok_ref.shape[2]
    hkv = NSA_KV_HEADS
    lo = [[jnp.zeros((nh, CMP_HIDDEN), F32) for _ in range(hkv)] for _ in range(2)]
    hi = [[jnp.zeros((nh, CMP_HIDDEN), F32) for _ in range(hkv)] for _ in range(2)]
    for l in range(CMP_STRIDE):
        for kv, x_ref in enumerate((xk_ref, xv_ref)):
            xg = x_ref[0, pl.ds(l, nh, stride=CMP_STRIDE), :]
            x_lo = (xg + pe_ref[kv, l:l + 1, :]).astype(BF16)
            x_hi = (xg + pe_ref[kv, CMP_STRIDE + l:CMP_STRIDE + l + 1, :]).astype(BF16)
            for hk in range(hkv):
                lo[kv][hk] = lo[kv][hk] + _dot(x_lo, w1_ref[kv, hk, l])
                hi[kv][hk] = hi[kv][hk] + _dot(x_hi, w1_ref[kv, hk, CMP_STRIDE + l])
    for hk in range(hkv):
        hid_k = lo[0][hk] + pltpu.roll(hi[0][hk], nh - 1, 0)
        hid_v = lo[1][hk] + pltpu.roll(hi[1][hk], nh - 1, 0)
        ok_ref[0, hk] = _dot(_silu(hid_k).astype(BF16), w2_ref[0]).astype(BF16)
        vt = jnp.transpose(_dot(_silu(hid_v).astype(BF16), w2_ref[1]))
        ov_ref[0, hk] = vt[0:NSA_HEAD_DIM].astype(BF16)


def _compress(xk, xv, pe, w1, w2, layer):
    b, seq, wide = xk.shape
    nh = seq // CMP_STRIDE
    hkv = NSA_KV_HEADS
    return pl.pallas_call(
        _compress_body,
        grid=(b,),
        in_specs=[pl.BlockSpec((1, seq, wide), lambda i: (i, 0, 0)),
                  pl.BlockSpec((1, seq, wide), lambda i: (i, 0, 0)),
                  _layer_spec(pe.shape[1:], layer, lambda i: (0, 0, 0)),
                  _layer_spec(w1.shape[1:], layer, lambda i: (0, 0, 0, 0, 0)),
                  _layer_spec(w2.shape[1:], layer, lambda i: (0, 0, 0))],
        out_specs=[pl.BlockSpec((1, hkv, nh, LANES), lambda i: (i, 0, 0, 0)),
                   pl.BlockSpec((1, hkv, NSA_HEAD_DIM, nh), lambda i: (i, 0, 0, 0))],
        out_shape=[jax.ShapeDtypeStruct((b, hkv, nh, LANES), BF16),
                   jax.ShapeDtypeStruct((b, hkv, NSA_HEAD_DIM, nh), BF16)],
        compiler_params=_cparams(("parallel",)),
        name="compress",
    )(xk, xv, pe, w1, w2)


def _colmax(s):
    r = s.shape[0]
    while r > 8 and r % 4 == 0:
        r //= 4
        s = jnp.max(s.reshape(4, r, s.shape[-1]), axis=0)
    return jnp.max(s, axis=0, keepdims=True)


def _nsa_body(q_ref, kc_ref, vc_ref, kaug_ref, vs_ref, kw_ref, vw_ref, g_ref, ovt_ref, cband_ref,
              wband_ref, o_ref, s_ref, b_ref, *, n_slc, n_sel, tk):
    i = pl.program_id(1)
    t0 = i * Q_BLOCK
    cols = NSA_GROUP * Q_BLOCK
    dh = NSA_HEAD_DIM
    heads = range(NSA_KV_HEADS)
    q_t = [q_ref[0, h, 0] for h in heads]
    lane = lax.broadcasted_iota(jnp.int32, (1, cols), 1)
    tq = t0 + (lane & (Q_BLOCK - 1))

    def softmax_cols(s, zero=None):
        m = _colmax(s)
        if zero is not None:
            m = m + zero
        e = jnp.exp2(s - m)
        return e, jnp.sum(e, axis=0, keepdims=True)

    nc = kc_ref.shape[2]
    per_q = Q_BLOCK // CMP_STRIDE
    cband = cband_ref[pl.ds(pl.multiple_of(nc - per_q * i, per_q), nc), :]
    wspan = Q_BLOCK + WINDOW
    c0 = jnp.maximum(i - WINDOW // Q_BLOCK, 0)
    start = pl.multiple_of(c0 * Q_BLOCK, Q_BLOCK)
    shift = pl.multiple_of(jnp.maximum(WINDOW - t0, 0), Q_BLOCK)
    wband = wband_ref[pl.ds(shift, wspan), :]
    o_cmp, o_win, imps = [], [], []
    for h in heads:
        e_c, l_c = softmax_cols(_dot(kc_ref[0, h], q_t[h]) + cband)
        inv_c = jnp.where(tq >= CMP_BLOCK - 1, 1.0 / l_c, 0.0)
        o_cmp.append(_dot(vc_ref[0, h], e_c.astype(BF16)) * inv_c)
        p_c = e_c * inv_c
        pg = p_c[:, 0:Q_BLOCK]
        for g in range(1, NSA_GROUP):
            pg = pg + p_c[:, g * Q_BLOCK:(g + 1) * Q_BLOCK]
        imp = _dot_exact_lhs(ovt_ref[...], pg)
        imps.append(imp)
        zero = jnp.concatenate([jnp.where(imp[0:1, :] > 1e30, 1.0, 0.0)] * NSA_GROUP, axis=1)
        e_w, l_w = softmax_cols(_dot(kw_ref[0, h, pl.ds(start, wspan), :], q_t[h]) + wband, zero)
        e_w = e_w.astype(BF16)
        ow = _dot(vw_ref[0, h, c0], e_w[0:Q_BLOCK])
        for c in range(1, wspan // Q_BLOCK):
            ow = ow + _dot(vw_ref[0, h, c0 + c], e_w[c * Q_BLOCK:(c + 1) * Q_BLOCK])
        o_win.append(ow * (1.0 / l_w))

    blk = lax.broadcasted_iota(jnp.int32, (LANES, Q_BLOCK), 0)
    tcol = t0 + lax.broadcasted_iota(jnp.int32, (LANES, Q_BLOCK), 1)
    cur = jnp.right_shift(tcol, SLC_BLOCK.bit_length() - 1)
    forced = (blk == 0) | (blk == cur) | (blk == cur - 1)
    valid = blk * SLC_BLOCK <= tcol
    score = [jnp.where(blk < n_slc, jnp.where(forced, FORCE_SCORE, jnp.where(valid, imps[h], -1.0)),
                       -jnp.inf) for h in heads]
    bias = [jnp.full((LANES, Q_BLOCK), SEL_BIAS, F32) for _ in heads]
    for _ in range(n_sel):
        for h in heads:
            mx = jnp.max(score[h], axis=0, keepdims=True)
            idx = jnp.min(jnp.where(score[h] == mx, blk, 2 * LANES), axis=0, keepdims=True)
            hit = blk == idx
            bias[h] = jnp.where(hit, 0.0, bias[h])
            score[h] = jnp.where(hit, -jnp.inf, score[h])
    for h in heads:
        b_ref[h] = jnp.concatenate([bias[h]] * NSA_GROUP, axis=1)

    nb = tk // SLC_BLOCK
    pad = jnp.zeros((16 - nb, cols), F32)

    def scores(h, j):
        rows = jnp.concatenate([b_ref[h, pl.ds(pl.multiple_of(j * nb, nb), nb), :], pad], axis=0)
        q_aug = jnp.concatenate([q_t[h][0:dh], rows.astype(BF16), q_t[h][dh + 16:]], axis=0)
        return _dot(kaug_ref[0, h, pl.ds(pl.multiple_of(j * tk, tk), tk), :], q_aug)

    def absorb(h, j, slot, m, acc, causal):
        s = s_ref[h, slot]
        if causal:
            kpos = j * tk + lax.broadcasted_iota(jnp.int32, (tk, 1), 0)
            s = jnp.where(kpos <= tq, s, NEG_INF)
        m_new = jnp.maximum(m, _colmax(s))
        p = jnp.exp2(s - m_new).astype(BF16)
        return m_new, jnp.exp2(m - m_new) * acc + _dot(vs_ref[0, h, j], p)

    def absorb_all(j, slot, carry, causal):
        out = ()
        for h in heads:
            out += absorb(h, j, slot, carry[2 * h], carry[2 * h + 1], causal)
        return out

    def fill(slot, j):
        for h in heads:
            s_ref[h, slot] = scores(h, j)

    def slc_pair(jj, carry):
        j = 2 * jj
        fill(1, j + 1)
        carry = absorb_all(j, 0, carry, False)
        fill(0, j + 2)
        return absorb_all(j + 1, 1, carry, False)

    def tail_odd(*carry):
        fill(1, n_full)
        carry = absorb_all(n_full - 1, 0, carry, False)
        return absorb_all(n_full, 1, carry, True)

    def tail_even(*carry):
        return absorb_all(n_full, 0, carry, True)

    n_full = t0 // tk
    fill(0, 0)
    init = (jnp.full((1, cols), M_INIT, F32), jnp.zeros((V_ROWS, cols), F32)) * NSA_KV_HEADS
    carry = lax.fori_loop(0, n_full // 2, slc_pair, init)
    carry = lax.cond(n_full % 2 == 1, tail_odd, tail_even, *carry)

    pairs = []
    for h in heads:
        acc_s = carry[2 * h + 1]
        o_slc = acc_s[0:dh] * (1.0 / acc_s[dh:dh + 1])
        gate = g_ref[0, 0, h * GATE_ROWS:(h + 1) * GATE_ROWS]
        outs = []
        for g in range(NSA_GROUP):
            sl = slice(g * Q_BLOCK, (g + 1) * Q_BLOCK)
            outs.append(gate[3 * g:3 * g + 1] * o_cmp[h][:, sl] + gate[3 * g + 1:3 * g + 2] * o_slc[:, sl]
                        + gate[3 * g + 2:3 * g + 3] * o_win[h][:, sl])
        pairs += [jnp.transpose(jnp.concatenate(outs[2 * k:2 * k + 2], axis=0))
                  for k in range(NSA_GROUP // 2)]
    o_ref[0] = jnp.concatenate(pairs, axis=1).astype(BF16)


def _nsa(q_t, kcmp, vcmp_t, kaug, vs_t, kw, vw_t, gates_t, ovt, cband, wband):
    b, hkv, nqb, _, cols = q_t.shape
    seq = kaug.shape[2]
    tk = vs_t.shape[4]
    n_slc = seq // SLC_BLOCK
    per_batch = lambda a: pl.BlockSpec((1,) + a.shape[1:], lambda bi, qi: (bi,) + (0,) * (a.ndim - 1))
    const = lambda a: pl.BlockSpec(a.shape, lambda bi, qi: (0, 0))
    body = functools.partial(_nsa_body, n_slc=n_slc, n_sel=min(N_SELECTED, n_slc), tk=tk)
    return pl.pallas_call(
        body,
        grid=(b, nqb),
        in_specs=[pl.BlockSpec((1, hkv, 1) + q_t.shape[3:], lambda bi, qi: (bi, 0, qi, 0, 0)),
                  per_batch(kcmp), per_batch(vcmp_t), per_batch(kaug), per_batch(vs_t),
                  per_batch(kw), per_batch(vw_t),
                  pl.BlockSpec((1, 1, SMALL_GATE, Q_BLOCK), lambda bi, qi: (bi, qi, 0, 0)),
                  const(ovt), const(cband), const(wband)],
        out_specs=pl.BlockSpec((1, Q_BLOCK, NSA_WIDTH), lambda bi, qi: (bi, qi, 0)),
        out_shape=jax.ShapeDtypeStruct((b, seq, NSA_WIDTH), BF16),
        scratch_shapes=[pltpu.VMEM((hkv, 2, tk, cols), F32), pltpu.VMEM((hkv, LANES, cols), F32)],
        compiler_params=_cparams(("parallel", "arbitrary")),
        name="nsa",
    )(q_t, kcmp, vcmp_t, kaug, vs_t, kw, vw_t, gates_t, ovt, cband, wband)


GDN_TILE = 2 * GDN_CHUNK


def _gdn_prep_body(qkv_ref, halo_ref, cw_ref, small_ref, alog_ref, dtb_ref,
                   u_ref, wq_ref, kdt_ref, attn_ref, eg_ref, *, ts):
    i = pl.program_id(1)
    x = qkv_ref[0]
    halo = jnp.where(i > 0, halo_ref[0], 0.0)
    xx = jnp.concatenate([halo, x], axis=0)
    y = x * cw_ref[CONV_WIDTH - 1:CONV_WIDTH, :]
    for d in range(1, CONV_WIDTH):
        shifted = pltpu.roll(xx, d, 0)[8:]
        y = y + shifted * cw_ref[CONV_WIDTH - 1 - d:CONV_WIDTH - d, :]
    y = _silu(y)

    sm = small_ref[0]
    sp_in = sm + dtb_ref[...]
    softplus = jnp.maximum(sp_in, 0.0) + jnp.log(1.0 + jnp.exp(-jnp.abs(sp_in)))
    glog = -jnp.exp(alog_ref[...]) * softplus

    ri = lax.broadcasted_iota(jnp.int32, (ts, ts), 0)
    ci = lax.broadcasted_iota(jnp.int32, (ts, ts), 1)
    sh = GDN_CHUNK.bit_length() - 1
    same = jnp.right_shift(ri, sh) == jnp.right_shift(ci, sh)
    tril = jnp.where(same & (ri >= ci), 1.0, 0.0).astype(BF16)
    ones = jnp.where(same, 1.0, 0.0).astype(BF16)
    gcum = _dot_exact_lhs(tril, glog)
    glast = _dot_exact_lhs(ones, glog)

    r2 = lax.broadcasted_iota(jnp.int32, (GDN_TILE, GDN_TILE), 0)
    c2 = lax.broadcasted_iota(jnp.int32, (GDN_TILE, GDN_TILE), 1)
    same2 = jnp.right_shift(r2, sh) == jnp.right_shift(c2, sh)
    incl = same2 & (r2 >= c2)
    strict = same2 & (r2 > c2)
    eye = jnp.where(r2 == c2, 1.0, 0.0)
    qscale = GDN_HEAD_DIM ** -0.5

    units = []
    for c in range(ts // GDN_TILE):
        r0 = c * GDN_TILE
        gc_tile = gcum[r0:r0 + GDN_TILE]
        gc_rows = jnp.transpose(gc_tile)
        for h in range(GDN_HEADS):
            lo = h * GDN_HEAD_DIM
            qh = y[r0:r0 + GDN_TILE, lo:lo + GDN_HEAD_DIM]
            kh = y[r0:r0 + GDN_TILE, GDN_WIDTH + lo:GDN_WIDTH + lo + GDN_HEAD_DIM]
            vh = y[r0:r0 + GDN_TILE, 2 * GDN_WIDTH + lo:2 * GDN_WIDTH + lo + GDN_HEAD_DIM]
            qh = qh * lax.rsqrt(jnp.sum(qh * qh, axis=-1, keepdims=True) + EPS)
            kh = kh * lax.rsqrt(jnp.sum(kh * kh, axis=-1, keepdims=True) + EPS)
            gc_col = gc_tile[:, SMALL_A + h:SMALL_A + h + 1]
            gc_row = gc_rows[SMALL_A + h:SMALL_A + h + 1, :]
            gl_col = glast[r0:r0 + GDN_TILE, SMALL_A + h:SMALL_A + h + 1]
            beta = sm[r0:r0 + GDN_TILE, SMALL_B + h:SMALL_B + h + 1]

            decay = jnp.where(incl, jnp.exp(jnp.minimum(gc_col - gc_row, 0.0)), 0.0)
            kb = kh * beta
            k_bf = kh.astype(BF16)
            a_s = jnp.where(strict, _dot_nt(kb.astype(BF16), k_bf) * decay, 0.0)
            egc = jnp.exp(gc_col)
            qs = qh * qscale
            attn = jnp.where(incl, _dot_nt(qs.astype(BF16), k_bf) * decay, 0.0)
            attn_ref[0, h, r0:r0 + GDN_TILE, :] = attn.astype(BF16)
            k_dec = kh * jnp.exp(gl_col - gc_col)
            kdt_ref[0, h, r0:r0 + GDN_TILE, :] = jnp.transpose(k_dec).astype(BF16)
            for cc in range(2):
                e0 = (r0 // GDN_CHUNK + cc) * 8
                eg_ref[0, h, e0:e0 + 8, :] = jnp.broadcast_to(
                    jnp.exp(gl_col[cc * GDN_CHUNK:cc * GDN_CHUNK + 8]), (8, GDN_HEAD_DIM))
            units.append(dict(h=h, r0=r0, a=a_s, rhs_u=vh * beta, rhs_w=kb * egc, q_dec=qs * egc))

    xinv = [eye - un['a'] for un in units]
    asp = [_split2(un['a']) for un in units]
    pw = [_dot_split(a, a) for a in asp]
    steps = GDN_CHUNK.bit_length() - 2
    for s in range(steps):
        pws = [_split2(p) for p in pw]
        xinv = [x + _dot_split(_split2(x), p) for x, p in zip(xinv, pws)]
        if s + 1 < steps:
            pw = [_dot_split(p, p) for p in pws]

    for un, x in zip(units, xinv):
        h, r0 = un['h'], un['r0']
        xs = _split2(x)
        u = _dot_split(xs, _split2(un['rhs_u']))
        w = _dot_split(xs, _split2(un['rhs_w']))
        u_ref[0, h, r0:r0 + GDN_TILE, :] = u
        for cc in range(2):
            a0 = cc * GDN_CHUNK
            wq = jnp.concatenate([w[a0:a0 + GDN_CHUNK], un['q_dec'][a0:a0 + GDN_CHUNK]], axis=0)
            n0 = 2 * r0 + cc * GDN_TILE
            wq_ref[0, h, n0:n0 + GDN_TILE, :] = wq.astype(BF16)


def _gdn_prep(qkv, cw, small, alog_row, dtb_row, layer, *, ts):
    b, seq, wide = qkv.shape
    nt = seq // ts
    hd = GDN_HEAD_DIM
    hspec = lambda rows: pl.BlockSpec((1, GDN_HEADS, rows, hd), lambda bi, ti: (bi, 0, ti, 0))
    hshape = lambda rows, dt: jax.ShapeDtypeStruct((b, GDN_HEADS, rows, hd), dt)
    return pl.pallas_call(
        functools.partial(_gdn_prep_body, ts=ts),
        grid=(b, nt),
        in_specs=[
            pl.BlockSpec((1, ts, wide), lambda bi, ti: (bi, ti, 0)),
            pl.BlockSpec((1, 8, wide), lambda bi, ti: (bi, jnp.maximum(ti * (ts // 8) - 1, 0), 0)),
            _layer_spec(cw.shape[1:], layer, lambda bi, ti: (0, 0)),
            pl.BlockSpec((1, ts, LANES), lambda bi, ti: (bi, ti, 0)),
            _layer_spec((1, LANES), layer, lambda bi, ti: (0, 0)),
            _layer_spec((1, LANES), layer, lambda bi, ti: (0, 0)),
        ],
        out_specs=[hspec(ts), hspec(2 * ts), hspec(ts), hspec(ts), hspec(ts // 8)],
        out_shape=[hshape(seq, F32), hshape(2 * seq, BF16), hshape(seq, BF16),
                   hshape(seq, BF16), hshape(seq // 8, F32)],
        compiler_params=_cparams(("parallel", "parallel")),
        name="gdn_prep",
    )(qkv, qkv, cw, small, alog_row, dtb_row)


def _gdn_scan_body(u_ref, wq_ref, kdt_ref, attn_ref, eg_ref, zg_ref, gn_ref, o_ref, st_ref, *, ts):
    @pl.when(pl.program_id(1) == 0)
    def _():
        st_ref[...] = jnp.zeros_like(st_ref)

    hd = GDN_HEAD_DIM
    gn = gn_ref[...]
    zeros = jnp.zeros((GDN_CHUNK, hd), F32)
    states = [st_ref[h] for h in range(GDN_HEADS)]
    for n in range(ts // GDN_CHUNK):
        r0 = n * GDN_CHUNK
        t0 = (n // 2) * GDN_TILE
        for h in range(GDN_HEADS):
            state = states[h]
            r = _dot(wq_ref[0, h, 2 * r0:2 * r0 + GDN_TILE, :], state.astype(BF16))
            v_new = u_ref[0, h, r0:r0 + GDN_CHUNK, :] - r[0:GDN_CHUNK]
            vpad = jnp.concatenate([v_new, zeros] if n % 2 == 0 else [zeros, v_new], axis=0)
            vpad = vpad.astype(BF16)
            o = r[GDN_CHUNK:] + _dot(attn_ref[0, h, r0:r0 + GDN_CHUNK, :], vpad)
            states[h] = state * eg_ref[0, h, 8 * n:8 * n + 1, :] + _dot(kdt_ref[0, h, t0:t0 + GDN_TILE, :], vpad)
            on = o * lax.rsqrt(jnp.mean(o * o, axis=-1, keepdims=True) + EPS) * gn
            gate = _silu(zg_ref[0, r0:r0 + GDN_CHUNK, h * hd:(h + 1) * hd])
            o_ref[0, r0:r0 + GDN_CHUNK, h * hd:(h + 1) * hd] = (on * gate).astype(BF16)
    for h in range(GDN_HEADS):
        st_ref[h] = states[h]


def _gdn_scan(u, wq, kdt, attn, eg, zg, gn, layer, *, ts):
    b, nh, seq, hd = u.shape
    hspec = lambda rows: pl.BlockSpec((1, nh, rows, hd), lambda bi, ti: (bi, 0, ti, 0))
    return pl.pallas_call(
        functools.partial(_gdn_scan_body, ts=ts),
        grid=(b, seq // ts),
        in_specs=[hspec(ts), hspec(2 * ts), hspec(ts), hspec(ts), hspec(ts // 8),
                  pl.BlockSpec((1, ts, nh * hd), lambda bi, ti: (bi, ti, 0)),
                  _layer_spec((1, hd), layer, lambda bi, ti: (0, 0))],
        out_specs=pl.BlockSpec((1, ts, nh * hd), lambda bi, ti: (bi, ti, 0)),
        out_shape=jax.ShapeDtypeStruct((b, seq, nh * hd), BF16),
        scratch_shapes=[pltpu.VMEM((nh, hd, hd), F32)],
        compiler_params=_cparams(("parallel", "arbitrary")),
        name="gdn_scan",
    )(u, wq, kdt, attn, eg, zg, gn)


def _outproj_body(x_ref, a_ref, b_ref, w_ref, o_ref):
    half = a_ref.shape[1]
    o_ref[...] = x_ref[...] + _dot(a_ref[...], w_ref[0:half, :]) + _dot(b_ref[...], w_ref[half:, :])


def _outproj(x, o_nsa, o_gdn, w_out, layer, *, tm):
    t, d = x.shape
    row = lambda i: (i, 0)
    return pl.pallas_call(
        _outproj_body,
        grid=(t // tm,),
        in_specs=[pl.BlockSpec((tm, d), row), pl.BlockSpec((tm, o_nsa.shape[1]), row),
                  pl.BlockSpec((tm, o_gdn.shape[1]), row),
                  _layer_spec(w_out.shape[1:], layer, lambda i: (0, 0))],
        out_specs=pl.BlockSpec((tm, d), row),
        out_shape=jax.ShapeDtypeStruct((t, d), F32),
        compiler_params=_cparams(("parallel",)),
        name="outproj",
    )(x, o_nsa, o_gdn, w_out)


def _ple_body(x_ref, p_ref, nw_ref, wg_ref, wp_ref, fn_ref, o_ref, *, final):
    x = x_ref[...]
    h = _rms(x, nw_ref[...]).astype(BF16)
    gate = _sigmoid(_dot(h, wg_ref[...]))
    out = x + gate * _dot(p_ref[...].astype(BF16), wp_ref[...])
    if final:
        out = _rms(out, fn_ref[...])
    o_ref[...] = out


def _ple(x, p, nw, wg, wp, fn, layer, *, tm, final):
    t, d = x.shape
    row = lambda i: (i, 0)
    return pl.pallas_call(
        functools.partial(_ple_body, final=final),
        grid=(t // tm,),
        in_specs=[pl.BlockSpec((tm, d), row), _layer_spec((tm, p.shape[2]), layer, row),
                  _layer_spec((1, d), layer, lambda i: (0, 0)),
                  _layer_spec(wg.shape[1:], layer, lambda i: (0, 0)),
                  _layer_spec(wp.shape[1:], layer, lambda i: (0, 0)),
                  pl.BlockSpec(fn.shape, lambda i: (0, 0))],
        out_specs=pl.BlockSpec((tm, d), row),
        out_shape=jax.ShapeDtypeStruct((t, d), F32),
        compiler_params=_cparams(("parallel",)),
        name="ple",
    )(x, p, nw, wg, wp, fn)


def _pack_w_in(w_in):
    offs = [0]
    for s in IN_SIZES:
        offs.append(offs[-1] + s)
    main = w_in[..., offs[0]:offs[7]]
    gates = w_in[..., offs[7]:offs[8]]
    gdn = w_in[..., offs[8]:offs[10]]
    ab = w_in[..., offs[10]:offs[12]]
    per_head = 3 * NSA_GROUP
    zpad = lambda n: jnp.zeros(w_in.shape[:-1] + (n,), w_in.dtype)
    gate_cols = []
    for hk in range(NSA_KV_HEADS):
        gate_cols += [gates[..., hk * per_head:(hk + 1) * per_head], zpad(GATE_ROWS - per_head)]
    tail = zpad(LANES - SMALL_GATE - ab.shape[-1])
    return jnp.concatenate([main, gdn] + gate_cols + [ab, tail], axis=-1).astype(BF16)


def _pack_cmp_w1(w1):
    depth = w1.shape[0]
    w1r = w1.reshape(depth, CMP_BLOCK, NSA_HEAD_DIM, CMP_HIDDEN).astype(BF16)
    z = jnp.zeros_like(w1r)
    return jnp.stack([jnp.concatenate([w1r, z], axis=2), jnp.concatenate([z, w1r], axis=2)], axis=1)


def _rope_tables(seq):
    dim = NSA_HEAD_DIM
    inv = 1.0 / (ROPE_THETA ** (jnp.arange(0, dim, 2, dtype=F32) / dim))
    ang = jnp.arange(seq, dtype=F32)[:, None] * inv[None, :]
    ang = jnp.concatenate([ang, ang], axis=-1)
    cos, sin = jnp.cos(ang), jnp.sin(ang)
    sign = jnp.where(jnp.arange(dim) < dim // 2, -1.0, 1.0).astype(F32)
    return jnp.tile(cos, (1, LANES // dim)), jnp.tile(sin * sign[None, :], (1, LANES // dim))


def _overlap_t(seq, nc_pad):
    n_slc = seq // SLC_BLOCK
    jc = jnp.arange(nc_pad)[None, :]
    js = jnp.arange(LANES)[:, None]
    ov = ((jc * CMP_STRIDE < (js + 1) * SLC_BLOCK) & (jc * CMP_STRIDE + CMP_BLOCK > js * SLC_BLOCK)
          & (js < n_slc))
    return ov.astype(BF16)


def _mask_bands(seq):
    q = (jnp.arange(NSA_GROUP * Q_BLOCK) % Q_BLOCK)[None, :]
    nc = seq // CMP_STRIDE
    rel = jnp.arange(2 * nc)[:, None] - nc
    cband = jnp.where(rel * CMP_STRIDE + (CMP_BLOCK - 1) <= q, 0.0, NEG_INF).astype(F32)
    r = jnp.arange(2 * WINDOW + Q_BLOCK)[:, None]
    wband = jnp.where((q < r) & (r <= q + WINDOW), 0.0, NEG_INF).astype(F32)
    return cband, wband


def _layer(x2, layer, w, consts, *, b, seq, final, cfg):
    t = b * seq
    x2 = _ffn(x2, w['ffn1_norm'], w['ffn1_w1'], w['ffn1_w3'], w['ffn1_w2'], layer,
              tm=cfg['ffn_tm'], tf=cfg['ffn_tf'])
    (q_t, kaug, vs_t, kw, vw_t, gates_t, cmpk, cmpv, qkv, zg, small) = _inproj(
        x2, w['mix_norm'], w['w_in'], consts['cos'], consts['sin'], layer, tm=cfg['in_tm'], b=b, seq=seq)
    kcmp, vcmp_t = _compress(cmpk.reshape(b, seq, LANES), cmpv.reshape(b, seq, LANES), w['cmp_pe'],
                             w['cmp_w1'], w['cmp_w2'], layer)
    o_nsa = _nsa(q_t, kcmp, vcmp_t, kaug, vs_t, kw, vw_t, gates_t, consts['ovt'], consts['cband'],
                 consts['wband'])
    u, wq, kdt, attn, eg = _gdn_prep(qkv.reshape(b, seq, 3 * GDN_WIDTH), w['gdn_conv'],
                                     small.reshape(b, seq, LANES), w['gdn_a_log'], w['gdn_dt_bias'],
                                     layer, ts=cfg['prep_ts'])
    o_gdn = _gdn_scan(u, wq, kdt, attn, eg, zg.reshape(b, seq, GDN_WIDTH), w['gdn_norm'], layer,
                      ts=cfg['scan_ts'])
    x2 = _outproj(x2, o_nsa.reshape(t, NSA_WIDTH), o_gdn.reshape(t, GDN_WIDTH), w['w_out'], layer,
                  tm=cfg['out_tm'])
    x2 = _ffn(x2, w['ffn2_norm'], w['ffn2_w1'], w['ffn2_w3'], w['ffn2_w2'], layer,
              tm=cfg['ffn_tm'], tf=cfg['ffn_tf'])
    return _ple(x2, w['p'], w['ple_norm'], w['ple_gate'], w['ple_proj'], consts['final_norm'], layer,
                tm=cfg['ple_tm'], final=final)


DEFAULT_CFG = dict(ffn_tm=512, ffn_tf=1408, in_tm=512, prep_ts=256, scan_ts=256,
                   out_tm=512, ple_tm=512)


def _forward(x, p, w, cfg):
    b, seq, d = x.shape
    depth = p.shape[0]
    t = b * seq
    cos2, sin2 = _rope_tables(seq)
    cband, wband = _mask_bands(seq)
    consts = dict(cos=cos2, sin=sin2, ovt=_overlap_t(seq, seq // CMP_STRIDE), cband=cband, wband=wband,
                  final_norm=w['final_norm'].reshape(1, d))
    bf = lambda a: a.astype(BF16)
    row3 = lambda a: a.reshape(depth, 1, a.shape[-1])
    lane_rows = lambda v, off: jnp.zeros((depth, 1, LANES), F32).at[:, 0, off:off + v.shape[1]].set(v)
    pe = jnp.stack([w['cmp_pe_k'], w['cmp_pe_v']], axis=1)
    w2 = jnp.stack([w['cmp_k_w2'], w['cmp_v_w2']], axis=1)
    ws = dict(
        p=p.reshape(depth, t, p.shape[-1]),
        ffn1_norm=row3(w['ffn1_norm']), ffn1_w1=bf(w['ffn1_w1']), ffn1_w3=bf(w['ffn1_w3']),
        ffn1_w2=bf(w['ffn1_w2']),
        mix_norm=row3(w['mix_norm']), w_in=_pack_w_in(w['w_in']),
        cmp_pe=jnp.concatenate([pe, pe], axis=-1),
        cmp_w1=jnp.stack([_pack_cmp_w1(w['cmp_k_w1']), _pack_cmp_w1(w['cmp_v_w1'])], axis=1),
        cmp_w2=bf(jnp.pad(w2, ((0, 0), (0, 0), (0, 0), (0, LANES - w2.shape[-1])))),
        gdn_conv=w['gdn_conv'], gdn_a_log=lane_rows(w['gdn_a_log'], SMALL_A),
        gdn_dt_bias=lane_rows(w['gdn_dt_bias'], SMALL_A), gdn_norm=row3(w['gdn_norm']),
        w_out=bf(w['w_out']),
        ffn2_norm=row3(w['ffn2_norm']), ffn2_w1=bf(w['ffn2_w1']), ffn2_w3=bf(w['ffn2_w3']),
        ffn2_w2=bf(w['ffn2_w2']),
        ple_norm=row3(w['ple_norm']), ple_gate=bf(w['ple_gate']), ple_proj=bf(w['ple_proj']),
    )
    x2 = x.reshape(t, d)
    for i in range(depth):
        x2 = _layer(x2, i, ws, consts, b=b, seq=seq, final=(i == depth - 1), cfg=cfg)
    return x2.reshape(b, seq, d)


def kernel(x, p, ffn1_norm, ffn1_w1, ffn1_w3, ffn1_w2, mix_norm, w_in, cmp_pe_k, cmp_pe_v,
           cmp_k_w1, cmp_k_w2, cmp_v_w1, cmp_v_w2, gdn_conv, gdn_a_log, gdn_dt_bias, gdn_norm,
           w_out, ffn2_norm, ffn2_w1, ffn2_w3, ffn2_w2, ple_norm, ple_gate, ple_proj, final_norm):
    w = dict(ffn1_norm=ffn1_norm, ffn1_w1=ffn1_w1, ffn1_w3=ffn1_w3, ffn1_w2=ffn1_w2,
             mix_norm=mix_norm, w_in=w_in, cmp_pe_k=cmp_pe_k, cmp_pe_v=cmp_pe_v,
             cmp_k_w1=cmp_k_w1, cmp_k_w2=cmp_k_w2, cmp_v_w1=cmp_v_w1, cmp_v_w2=cmp_v_w2,
             gdn_conv=gdn_conv, gdn_a_log=gdn_a_log, gdn_dt_bias=gdn_dt_bias, gdn_norm=gdn_norm,
             w_out=w_out, ffn2_norm=ffn2_norm, ffn2_w1=ffn2_w1, ffn2_w3=ffn2_w3, ffn2_w2=ffn2_w2,
             ple_norm=ple_norm, ple_gate=ple_gate, ple_proj=ple_proj, final_norm=final_norm)
    return _forward(x, p, w, DEFAULT_CFG)
```

```python
import functools

import jax
import jax.numpy as jnp
from jax import lax
from jax.experimental import pallas as pl
from jax.experimental.pallas import tpu as pltpu

F32 = jnp.float32
BF16 = jnp.bfloat16

D_MODEL = 1024
NSA_HEADS = 8
NSA_KV_HEADS = 2
NSA_GROUP = NSA_HEADS // NSA_KV_HEADS
NSA_HEAD_DIM = 64
CMP_BLOCK = 32
CMP_STRIDE = 16
CMP_HIDDEN = 128
SLC_BLOCK = 64
N_SELECTED = 16
WINDOW = 512
Q_BLOCK = 128
GDN_HEADS = 4
GDN_HEAD_DIM = 128
GDN_CHUNK = 64
CONV_WIDTH = 4
D_FF = 2816
PLE_DIM = 256
ROPE_THETA = 10000.0
EPS = 1e-6
FORCE_SCORE = 1e6
NEG_INF = -1e30

NSA_WIDTH = NSA_HEADS * NSA_HEAD_DIM
NSA_KV_WIDTH = NSA_KV_HEADS * NSA_HEAD_DIM
GDN_WIDTH = GDN_HEADS * GDN_HEAD_DIM
IN_SIZES = (NSA_WIDTH, NSA_KV_WIDTH, NSA_KV_WIDTH, NSA_KV_WIDTH, NSA_KV_WIDTH,
            NSA_KV_WIDTH, NSA_KV_WIDTH, 3 * NSA_HEADS, 3 * GDN_WIDTH, GDN_WIDTH,
            GDN_HEADS, GDN_HEADS)

LANES = 128
NSA_MAIN = NSA_WIDTH + 6 * NSA_KV_WIDTH
GDN_MAIN = 4 * GDN_WIDTH
GATE_ROWS = 16
SMALL_GATE = NSA_KV_HEADS * GATE_ROWS
SMALL_A = SMALL_GATE
SMALL_B = SMALL_GATE + GDN_HEADS
W_IN_PACKED = NSA_MAIN + GDN_MAIN + LANES
V_ROWS = NSA_HEAD_DIM + 16
SEL_BIAS = -2.0 ** 100
M_INIT = -3.0e38
LOG2_E = 1.4426950408889634
VMEM_LIMIT = 56 * 1024 * 1024


def _cparams(sem):
    return pltpu.CompilerParams(dimension_semantics=sem, vmem_limit_bytes=VMEM_LIMIT)


def _rms(x, w):
    ms = jnp.mean(x * x, axis=-1, keepdims=True)
    return x * lax.rsqrt(ms + EPS) * w


def _sigmoid(x):
    return 1.0 / (1.0 + jnp.exp(-x))


def _silu(x):
    return x * _sigmoid(x)


def _dot(a, b):
    return jnp.dot(a, b, preferred_element_type=F32)


def _dot_nt(a, b):
    return lax.dot_general(a, b, (((1,), (1,)), ((), ())), preferred_element_type=F32)


def _split3(x):
    hi = x.astype(BF16)
    r = x - hi.astype(F32)
    mid = r.astype(BF16)
    lo = (r - mid.astype(F32)).astype(BF16)
    return hi, mid, lo


def _dot_exact_lhs(a_bf, x):
    hi, mid, lo = _split3(x)
    return _dot(a_bf, hi) + (_dot(a_bf, mid) + _dot(a_bf, lo))


def _split2(x):
    hi = x.astype(BF16)
    return hi, (x - hi.astype(F32)).astype(BF16)


def _dot_split(a, b):
    ah, am = a
    bh, bm = b
    return _dot(ah, bh) + (_dot(ah, bm) + _dot(am, bh))


def _layer_spec(block, layer, index):
    return pl.BlockSpec((None,) + block, lambda *g: (layer,) + index(*g))


def _ffn_body(x_ref, nw_ref, w1_ref, w3_ref, w2_ref, o_ref, h_ref):
    j = pl.program_id(1)

    @pl.when(j == 0)
    def _():
        h_ref[...] = _rms(x_ref[...], nw_ref[...]).astype(BF16)
        o_ref[...] = jnp.zeros_like(o_ref)

    h = h_ref[...]
    u = _dot(h, w1_ref[...])
    g = _dot(h, w3_ref[...])
    a = (_silu(u) * g).astype(BF16)
    o_ref[...] += _dot(a, w2_ref[...])

    @pl.when(j == pl.num_programs(1) - 1)
    def _():
        o_ref[...] = x_ref[...] + 0.5 * o_ref[...]


def _ffn(x, nw, w1, w3, w2, layer, *, tm, tf):
    t, d = x.shape
    ff = w1.shape[2]
    return pl.pallas_call(
        _ffn_body,
        grid=(t // tm, ff // tf),
        in_specs=[
            pl.BlockSpec((tm, d), lambda i, j: (i, 0)),
            _layer_spec((1, d), layer, lambda i, j: (0, 0)),
            _layer_spec((d, tf), layer, lambda i, j: (0, j)),
            _layer_spec((d, tf), layer, lambda i, j: (0, j)),
            _layer_spec((tf, d), layer, lambda i, j: (j, 0)),
        ],
        out_specs=pl.BlockSpec((tm, d), lambda i, j: (i, 0)),
        out_shape=jax.ShapeDtypeStruct((t, d), F32),
        scratch_shapes=[pltpu.VMEM((tm, d), BF16)],
        compiler_params=_cparams(("parallel", "arbitrary")),
        name="ffn",
    )(x, nw, w1, w3, w2)


def _rope(xg, cos, sin_signed, first_half):
    fwd = pltpu.roll(xg, LANES - NSA_HEAD_DIM // 2, 1)
    bwd = pltpu.roll(xg, NSA_HEAD_DIM // 2, 1)
    return xg * cos + jnp.where(first_half, fwd, bwd) * sin_signed


def _inproj_body(x_ref, nw_ref, w_ref, cos_ref, sin_ref,
                 qt_ref, kaug_ref, vst_ref, kw_ref, vwt_ref, gt_ref,
                 cmpk_ref, cmpv_ref, qkv_ref, zg_ref, small_ref, *, nseq):
    tm = x_ref.shape[0]
    nq = tm // Q_BLOCK
    dh = NSA_HEAD_DIM
    h = _rms(x_ref[...], nw_ref[...]).astype(BF16)
    cos = cos_ref[...]
    sin_s = sin_ref[...]
    lane = lax.broadcasted_iota(jnp.int32, (1, LANES), 1)
    first_half = (lane & (dh - 1)) < (dh // 2)
    low = lane < dh

    z = _dot(h, w_ref[:, 0:NSA_MAIN])
    scale = dh ** -0.5 * LOG2_E
    for pair in range(NSA_WIDTH // LANES):
        zq = _rope(z[:, pair * LANES:(pair + 1) * LANES], cos, sin_s, first_half) * scale
        tr = jnp.transpose(zq).astype(BF16)
        for half in range(2):
            hk, g = divmod(2 * pair + half, NSA_GROUP)
            for qb in range(nq):
                qt_ref[0, hk, qb, 0:dh, g * Q_BLOCK:(g + 1) * Q_BLOCK] = (
                    tr[half * dh:(half + 1) * dh, qb * Q_BLOCK:(qb + 1) * Q_BLOCK])
    qt_ref[0, :, :, dh:, :] = jnp.zeros((NSA_KV_HEADS, nq, LANES - dh, NSA_GROUP * Q_BLOCK), BF16)

    def group(c, rotary):
        zc = z[:, NSA_WIDTH + c * LANES: NSA_WIDTH + (c + 1) * LANES]
        return _rope(zc, cos, sin_s, first_half) if rotary else zc

    def heads(zc):
        return [jnp.where(low, zc, 0.0), jnp.where(low, pltpu.roll(zc, dh, 1), 0.0)]

    cmpk_ref[...] = group(0, True)
    cmpv_ref[...] = group(1, False)
    tok = (pl.program_id(0) % nseq) * tm + lax.broadcasted_iota(jnp.int32, (tm, 1), 0)
    local_blk = jnp.right_shift(tok, SLC_BLOCK.bit_length() - 1) & (tm // SLC_BLOCK - 1)
    onehot = jnp.where(lane - dh == local_blk, 1.0, 0.0)
    for hk, kh in enumerate(heads(group(2, True))):
        kaug_ref[0, hk] = jnp.where(low, kh, onehot).astype(BF16)
    for hk, kh in enumerate(heads(group(4, True))):
        kw_ref[0, hk] = kh.astype(BF16)
    vst = jnp.transpose(group(3, False)).astype(BF16)
    ones_rows = jnp.where(lax.broadcasted_iota(jnp.int32, (V_ROWS - dh, tm), 0) == 0, 1.0, 0.0)
    vwt = jnp.transpose(group(5, False)).astype(BF16)
    for hk in range(NSA_KV_HEADS):
        vst_ref[0, hk, 0] = jnp.concatenate([vst[hk * dh:(hk + 1) * dh], ones_rows.astype(BF16)], axis=0)
        for qb in range(nq):
            vwt_ref[0, hk, qb] = vwt[hk * dh:(hk + 1) * dh, qb * Q_BLOCK:(qb + 1) * Q_BLOCK]

    zg = _dot(h, w_ref[:, NSA_MAIN:NSA_MAIN + GDN_MAIN])
    qkv_ref[...] = zg[:, 0:3 * GDN_WIDTH]
    zg_ref[...] = zg[:, 3 * GDN_WIDTH:]

    zs = _dot(h, w_ref[:, NSA_MAIN + GDN_MAIN:])
    is_raw = (lane >= SMALL_A) & (lane < SMALL_B)
    small = jnp.where(is_raw, zs, _sigmoid(zs))
    small_ref[...] = small
    small_t = jnp.transpose(small)
    for qb in range(nq):
        gt_ref[0, qb] = small_t[0:SMALL_GATE, qb * Q_BLOCK:(qb + 1) * Q_BLOCK]


def _inproj(x, nw, w_packed, cos2, sin2, layer, *, tm, b, seq):
    t, d = x.shape
    nseq = seq // tm
    nq = tm // Q_BLOCK
    nqb = seq // Q_BLOCK
    hkv, dh = NSA_KV_HEADS, NSA_HEAD_DIM
    cols = NSA_GROUP * Q_BLOCK
    row = lambda i: (i, 0)
    tile5 = lambda i: (i // nseq, 0, i % nseq, 0, 0)
    tile4 = lambda i: (i // nseq, 0, i % nseq, 0)
    return pl.pallas_call(
        functools.partial(_inproj_body, nseq=nseq),
        grid=(t // tm,),
        in_specs=[
            pl.BlockSpec((tm, d), row),
            _layer_spec((1, d), layer, lambda i: (0, 0)),
            _layer_spec((d, W_IN_PACKED), layer, lambda i: (0, 0)),
            pl.BlockSpec((tm, LANES), lambda i: (i % nseq, 0)),
            pl.BlockSpec((tm, LANES), lambda i: (i % nseq, 0)),
        ],
        out_specs=[
            pl.BlockSpec((1, hkv, nq, LANES, cols), tile5),
            pl.BlockSpec((1, hkv, tm, LANES), tile4),
            pl.BlockSpec((1, hkv, 1, V_ROWS, tm), tile5),
            pl.BlockSpec((1, hkv, tm, LANES), tile4),
            pl.BlockSpec((1, hkv, nq, dh, Q_BLOCK), tile5),
            pl.BlockSpec((1, nq, SMALL_GATE, Q_BLOCK), lambda i: (i // nseq, i % nseq, 0, 0)),
            pl.BlockSpec((tm, NSA_KV_WIDTH), row),
            pl.BlockSpec((tm, NSA_KV_WIDTH), row),
            pl.BlockSpec((tm, 3 * GDN_WIDTH), row),
            pl.BlockSpec((tm, GDN_WIDTH), row),
            pl.BlockSpec((tm, LANES), row),
        ],
        out_shape=[
            jax.ShapeDtypeStruct((b, hkv, nqb, LANES, cols), BF16),
            jax.ShapeDtypeStruct((b, hkv, seq, LANES), BF16),
            jax.ShapeDtypeStruct((b, hkv, nseq, V_ROWS, tm), BF16),
            jax.ShapeDtypeStruct((b, hkv, seq, LANES), BF16),
            jax.ShapeDtypeStruct((b, hkv, nqb, dh, Q_BLOCK), BF16),
            jax.ShapeDtypeStruct((b, nqb, SMALL_GATE, Q_BLOCK), F32),
            jax.ShapeDtypeStruct((t, NSA_KV_WIDTH), F32),
            jax.ShapeDtypeStruct((t, NSA_KV_WIDTH), F32),
            jax.ShapeDtypeStruct((t, 3 * GDN_WIDTH), F32),
            jax.ShapeDtypeStruct((t, GDN_WIDTH), F32),
            jax.ShapeDtypeStruct((t, LANES), F32),
        ],
        compiler_params=_cparams(("parallel",)),
        name="inproj",
    )(x, nw, w_packed, cos2, sin2)


def _compress_body(xk_ref, xv_ref, pe_ref, w1_ref, w2_ref, ok_ref, ov_ref):
    nh = ok_ref.shape[2]
    hkv = NSA_KV_HEADS
    lo = [[jnp.zeros((nh, CMP_HIDDEN), F32) for _ in range(hkv)] for _ in range(2)]
    hi = [[jnp.zeros((nh, CMP_HIDDEN), F32) for _ in range(hkv)] for _ in range(2)]
    for l in range(CMP_STRIDE):
        for kv, x_ref in enumerate((xk_ref, xv_ref)):
            xg = x_ref[0, pl.ds(l, nh, stride=CMP_STRIDE), :]
            x_lo = (xg + pe_ref[kv, l:l + 1, :]).astype(BF16)
            x_hi = (xg + pe_ref[kv, CMP_STRIDE + l:CMP_STRIDE + l + 1, :]).astype(BF16)
            for hk in range(hkv):
                lo[kv][hk] = lo[kv][hk] + _dot(x_lo, w1_ref[kv, hk, l])
                hi[kv][hk] = hi[kv][hk] + _dot(x_hi, w1_ref[kv, hk, CMP_STRIDE + l])
    for hk in range(hkv):
        hid_k = lo[0][hk] + pltpu.roll(hi[0][hk], nh - 1, 0)
        hid_v = lo[1][hk] + pltpu.roll(hi[1][hk], nh - 1, 0)
        ok_ref[0, hk] = _dot(_silu(hid_k).astype(BF16), w2_ref[0]).astype(BF16)
        vt = jnp.transpose(_dot(_silu(hid_v).astype(BF16), w2_ref[1]))
        ov_ref[0, hk] = vt[0:NSA_HEAD_DIM].astype(BF16)


def _compress(xk, xv, pe, w1, w2, layer):
    b, seq, wide = xk.shape
    nh = seq // CMP_STRIDE
    hkv = NSA_KV_HEADS
    return pl.pallas_call(
        _compress_body,
        grid=(b,),
        in_specs=[pl.BlockSpec((1, seq, wide), lambda i: (i, 0, 0)),
                  pl.BlockSpec((1, seq, wide), lambda i: (i, 0, 0)),
                  _layer_spec(pe.shape[1:], layer, lambda i: (0, 0, 0)),
                  _layer_spec(w1.shape[1:], layer, lambda i: (0, 0, 0, 0, 0)),
                  _layer_spec(w2.shape[1:], layer, lambda i: (0, 0, 0))],
        out_specs=[pl.BlockSpec((1, hkv, nh, LANES), lambda i: (i, 0, 0, 0)),
                   pl.BlockSpec((1, hkv, NSA_HEAD_DIM, nh), lambda i: (i, 0, 0, 0))],
        out_shape=[jax.ShapeDtypeStruct((b, hkv, nh, LANES), BF16),
                   jax.ShapeDtypeStruct((b, hkv, NSA_HEAD_DIM, nh), BF16)],
        compiler_params=_cparams(("parallel",)),
        name="compress",
    )(xk, xv, pe, w1, w2)


def _colmax(s):
    r = s.shape[0]
    while r > 8 and r % 4 == 0:
        r //= 4
        s = jnp.max(s.reshape(4, r, s.shape[-1]), axis=0)
    return jnp.max(s, axis=0, keepdims=True)


def _nsa_body(q_ref, kc_ref, vc_ref, kaug_ref, vs_ref, kw_ref, vw_ref, g_ref, ovt_ref, cband_ref,
              wband_ref, o_ref, s_ref, b_ref, *, n_slc, n_sel, tk):
    i = pl.program_id(1)
    t0 = i * Q_BLOCK
    cols = NSA_GROUP * Q_BLOCK
    dh = NSA_HEAD_DIM
    heads = range(NSA_KV_HEADS)
    q_t = [q_ref[0, h, 0] for h in heads]
    lane = lax.broadcasted_iota(jnp.int32, (1, cols), 1)
    tq = t0 + (lane & (Q_BLOCK - 1))

    def softmax_cols(s, zero=None):
        m = _colmax(s)
        if zero is not None:
            m = m + zero
        e = jnp.exp2(s - m)
        return e, jnp.sum(e, axis=0, keepdims=True)

    nc = kc_ref.shape[2]
    per_q = Q_BLOCK // CMP_STRIDE
    cband = cband_ref[pl.ds(pl.multiple_of(nc - per_q * i, per_q), nc), :]
    wspan = Q_BLOCK + WINDOW
    c0 = jnp.maximum(i - WINDOW // Q_BLOCK, 0)
    start = pl.multiple_of(c0 * Q_BLOCK, Q_BLOCK)
    shift = pl.multiple_of(jnp.maximum(WINDOW - t0, 0), Q_BLOCK)
    wband = wband_ref[pl.ds(shift, wspan), :]
    o_cmp, o_win, imps = [], [], []
    for h in heads:
        e_c, l_c = softmax_cols(_dot(kc_ref[0, h], q_t[h]) + cband)
        inv_c = jnp.where(tq >= CMP_BLOCK - 1, 1.0 / l_c, 0.0)
        o_cmp.append(_dot(vc_ref[0, h], e_c.astype(BF16)) * inv_c)
        p_c = e_c * inv_c
        pg = p_c[:, 0:Q_BLOCK]
        for g in range(1, NSA_GROUP):
            pg = pg + p_c[:, g * Q_BLOCK:(g + 1) * Q_BLOCK]
        imp = _dot_exact_lhs(ovt_ref[...], pg)
        imps.append(imp)
        zero = jnp.concatenate([jnp.where(imp[0:1, :] > 1e30, 1.0, 0.0)] * NSA_GROUP, axis=1)
        e_w, l_w = softmax_cols(_dot(kw_ref[0, h, pl.ds(start, wspan), :], q_t[h]) + wband, zero)
        e_w = e_w.astype(BF16)
        ow = _dot(vw_ref[0, h, c0], e_w[0:Q_BLOCK])
        for c in range(1, wspan // Q_BLOCK):
            ow = ow + _dot(vw_ref[0, h, c0 + c], e_w[c * Q_BLOCK:(c + 1) * Q_BLOCK])
        o_win.append(ow * (1.0 / l_w))

    blk = lax.broadcasted_iota(jnp.int32, (LANES, Q_BLOCK), 0)
    tcol = t0 + lax.broadcasted_iota(jnp.int32, (LANES, Q_BLOCK), 1)
    cur = jnp.right_shift(tcol, SLC_BLOCK.bit_length() - 1)
    forced = (blk == 0) | (blk == cur) | (blk == cur - 1)
    valid = blk * SLC_BLOCK <= tcol
    score = [jnp.where(blk < n_slc, jnp.where(forced, FORCE_SCORE, jnp.where(valid, imps[h], -1.0)),
                       -jnp.inf) for h in heads]
    bias = [jnp.full((LANES, Q_BLOCK), SEL_BIAS, F32) for _ in heads]
    for _ in range(n_sel):
        for h in heads:
            mx = jnp.max(score[h], axis=0, keepdims=True)
            idx = jnp.min(jnp.where(score[h] == mx, blk, 2 * LANES), axis=0, keepdims=True)
            hit = blk == idx
            bias[h] = jnp.where(hit, 0.0, bias[h])
            score[h] = jnp.where(hit, -jnp.inf, score[h])
    for h in heads:
        b_ref[h] = jnp.concatenate([bias[h]] * NSA_GROUP, axis=1)

    nb = tk // SLC_BLOCK
    pad = jnp.zeros((16 - nb, cols), F32)

    def scores(h, j):
        rows = jnp.concatenate([b_ref[h, pl.ds(pl.multiple_of(j * nb, nb), nb), :], pad], axis=0)
        q_aug = jnp.concatenate([q_t[h][0:dh], rows.astype(BF16), q_t[h][dh + 16:]], axis=0)
        return _dot(kaug_ref[0, h, pl.ds(pl.multiple_of(j * tk, tk), tk), :], q_aug)

    def absorb(h, j, slot, m, acc, causal):
        s = s_ref[h, slot]
        if causal:
            kpos = j * tk + lax.broadcasted_iota(jnp.int32, (tk, 1), 0)
            s = jnp.where(kpos <= tq, s, NEG_INF)
        m_new = jnp.maximum(m, _colmax(s))
        p = jnp.exp2(s - m_new).astype(BF16)
        return m_new, jnp.exp2(m - m_new) * acc + _dot(vs_ref[0, h, j], p)

    def absorb_all(j, slot, carry, causal):
        out = ()
        for h in heads:
            out += absorb(h, j, slot, carry[2 * h], carry[2 * h + 1], causal)
        return out

    def fill(slot, j):
        for h in heads:
            s_ref[h, slot] = scores(h, j)

    def slc_pair(jj, carry):
        j = 2 * jj
        fill(1, j + 1)
        carry = absorb_all(j, 0, carry, False)
        fill(0, j + 2)
        return absorb_all(j + 1, 1, carry, False)

    def tail_odd(*carry):
        fill(1, n_full)
        carry = absorb_all(n_full - 1, 0, carry, False)
        return absorb_all(n_full, 1, carry, True)

    def tail_even(*carry):
        return absorb_all(n_full, 0, carry, True)

    n_full = t0 // tk
    fill(0, 0)
    init = (jnp.full((1, cols), M_INIT, F32), jnp.zeros((V_ROWS, cols), F32)) * NSA_KV_HEADS
    carry = lax.fori_loop(0, n_full // 2, slc_pair, init)
    carry = lax.cond(n_full % 2 == 1, tail_odd, tail_even, *carry)

    pairs = []
    for h in heads:
        acc_s = carry[2 * h + 1]
        o_slc = acc_s[0:dh] * (1.0 / acc_s[dh:dh + 1])
        gate = g_ref[0, 0, h * GATE_ROWS:(h + 1) * GATE_ROWS]
        outs = []
        for g in range(NSA_GROUP):
            sl = slice(g * Q_BLOCK, (g + 1) * Q_BLOCK)
            outs.append(gate[3 * g:3 * g + 1] * o_cmp[h][:, sl] + gate[3 * g + 1:3 * g + 2] * o_slc[:, sl]
                        + gate[3 * g + 2:3 * g + 3] * o_win[h][:, sl])
        pairs += [jnp.transpose(jnp.concatenate(outs[2 * k:2 * k + 2], axis=0))
                  for k in range(NSA_GROUP // 2)]
    o_ref[0] = jnp.concatenate(pairs, axis=1).astype(BF16)


def _nsa(q_t, kcmp, vcmp_t, kaug, vs_t, kw, vw_t, gates_t, ovt, cband, wband):
    b, hkv, nqb, _, cols = q_t.shape
    seq = kaug.shape[2]
    tk = vs_t.shape[4]
    n_slc = seq // SLC_BLOCK
    per_batch = lambda a: pl.BlockSpec((1,) + a.shape[1:], lambda bi, qi: (bi,) + (0,) * (a.ndim - 1))
    const = lambda a: pl.BlockSpec(a.shape, lambda bi, qi: (0, 0))
    body = functools.partial(_nsa_body, n_slc=n_slc, n_sel=min(N_SELECTED, n_slc), tk=tk)
    return pl.pallas_call(
        body,
        grid=(b, nqb),
        in_specs=[pl.BlockSpec((1, hkv, 1) + q_t.shape[3:], lambda bi, qi: (bi, 0, qi, 0, 0)),
                  per_batch(kcmp), per_batch(vcmp_t), per_batch(kaug), per_batch(vs_t),
                  per_batch(kw), per_batch(vw_t),
                  pl.BlockSpec((1, 1, SMALL_GATE, Q_BLOCK), lambda bi, qi: (bi, qi, 0, 0)),
                  const(ovt), const(cband), const(wband)],
        out_specs=pl.BlockSpec((1, Q_BLOCK, NSA_WIDTH), lambda bi, qi: (bi, qi, 0)),
        out_shape=jax.ShapeDtypeStruct((b, seq, NSA_WIDTH), BF16),
        scratch_shapes=[pltpu.VMEM((hkv, 2, tk, cols), F32), pltpu.VMEM((hkv, LANES, cols), F32)],
        compiler_params=_cparams(("parallel", "arbitrary")),
        name="nsa",
    )(q_t, kcmp, vcmp_t, kaug, vs_t, kw, vw_t, gates_t, ovt, cband, wband)


GDN_TILE = 2 * GDN_CHUNK


def _gdn_prep_body(qkv_ref, halo_ref, cw_ref, small_ref, alog_ref, dtb_ref,
                   u_ref, wq_ref, kdt_ref, attn_ref, eg_ref, *, ts):
    i = pl.program_id(1)
    x = qkv_ref[0]
    halo = jnp.where(i > 0, halo_ref[0], 0.0)
    xx = jnp.concatenate([halo, x], axis=0)
    y = x * cw_ref[CONV_WIDTH - 1:CONV_WIDTH, :]
    for d in range(1, CONV_WIDTH):
        shifted = pltpu.roll(xx, d, 0)[8:]
        y = y + shifted * cw_ref[CONV_WIDTH - 1 - d:CONV_WIDTH - d, :]
    y = _silu(y)

    sm = small_ref[0]
    sp_in = sm + dtb_ref[...]
    softplus = jnp.maximum(sp_in, 0.0) + jnp.log(1.0 + jnp.exp(-jnp.abs(sp_in)))
    glog = -jnp.exp(alog_ref[...]) * softplus

    ri = lax.broadcasted_iota(jnp.int32, (ts, ts), 0)
    ci = lax.broadcasted_iota(jnp.int32, (ts, ts), 1)
    sh = GDN_CHUNK.bit_length() - 1
    same = jnp.right_shift(ri, sh) == jnp.right_shift(ci, sh)
    tril = jnp.where(same & (ri >= ci), 1.0, 0.0).astype(BF16)
    ones = jnp.where(same, 1.0, 0.0).astype(BF16)
    gcum = _dot_exact_lhs(tril, glog)
    glast = _dot_exact_lhs(ones, glog)

    r2 = lax.broadcasted_iota(jnp.int32, (GDN_TILE, GDN_TILE), 0)
    c2 = lax.broadcasted_iota(jnp.int32, (GDN_TILE, GDN_TILE), 1)
    same2 = jnp.right_shift(r2, sh) == jnp.right_shift(c2, sh)
    incl = same2 & (r2 >= c2)
    strict = same2 & (r2 > c2)
    eye = jnp.where(r2 == c2, 1.0, 0.0)
    qscale = GDN_HEAD_DIM ** -0.5

    units = []
    for c in range(ts // GDN_TILE):
        r0 = c * GDN_TILE
        gc_tile = gcum[r0:r0 + GDN_TILE]
        gc_rows = jnp.transpose(gc_tile)
        for h in range(GDN_HEADS):
            lo = h * GDN_HEAD_DIM
            qh = y[r0:r0 + GDN_TILE, lo:lo + GDN_HEAD_DIM]
            kh = y[r0:r0 + GDN_TILE, GDN_WIDTH + lo:GDN_WIDTH + lo + GDN_HEAD_DIM]
            vh = y[r0:r0 + GDN_TILE, 2 * GDN_WIDTH + lo:2 * GDN_WIDTH + lo + GDN_HEAD_DIM]
            qh = qh * lax.rsqrt(jnp.sum(qh * qh, axis=-1, keepdims=True) + EPS)
            kh = kh * lax.rsqrt(jnp.sum(kh * kh, axis=-1, keepdims=True) + EPS)
            gc_col = gc_tile[:, SMALL_A + h:SMALL_A + h + 1]
            gc_row = gc_rows[SMALL_A + h:SMALL_A + h + 1, :]
            gl_col = glast[r0:r0 + GDN_TILE, SMALL_A + h:SMALL_A + h + 1]
            beta = sm[r0:r0 + GDN_TILE, SMALL_B + h:SMALL_B + h + 1]

            decay = jnp.where(incl, jnp.exp(jnp.minimum(gc_col - gc_row, 0.0)), 0.0)
            kb = kh * beta
            k_bf = kh.astype(BF16)
            a_s = jnp.where(strict, _dot_nt(kb.astype(BF16), k_bf) * decay, 0.0)
            egc = jnp.exp(gc_col)
            qs = qh * qscale
            attn = jnp.where(incl, _dot_nt(qs.astype(BF16), k_bf) * decay, 0.0)
            attn_ref[0, h, r0:r0 + GDN_TILE, :] = attn.astype(BF16)
            k_dec = kh * jnp.exp(gl_col - gc_col)
            kdt_ref[0, h, r0:r0 + GDN_TILE, :] = jnp.transpose(k_dec).astype(BF16)
            for cc in range(2):
                e0 = (r0 // GDN_CHUNK + cc) * 8
                eg_ref[0, h, e0:e0 + 8, :] = jnp.broadcast_to(
                    jnp.exp(gl_col[cc * GDN_CHUNK:cc * GDN_CHUNK + 8]), (8, GDN_HEAD_DIM))
            units.append(dict(h=h, r0=r0, a=a_s, rhs_u=vh * beta, rhs_w=kb * egc, q_dec=qs * egc))

    xinv = [eye - un['a'] for un in units]
    asp = [_split2(un['a']) for un in units]
    pw = [_dot_split(a, a) for a in asp]
    steps = GDN_CHUNK.bit_length() - 2
    for s in range(steps):
        pws = [_split2(p) for p in pw]
        xinv = [x + _dot_split(_split2(x), p) for x, p in zip(xinv, pws)]
        if s + 1 < steps:
            pw = [_dot_split(p, p) for p in pws]

    for un, x in zip(units, xinv):
        h, r0 = un['h'], un['r0']
        xs = _split2(x)
        u = _dot_split(xs, _split2(un['rhs_u']))
        w = _dot_split(xs, _split2(un['rhs_w']))
        u_ref[0, h, r0:r0 + GDN_TILE, :] = u
        for cc in range(2):
            a0 = cc * GDN_CHUNK
            wq = jnp.concatenate([w[a0:a0 + GDN_CHUNK], un['q_dec'][a0:a0 + GDN_CHUNK]], axis=0)
            n0 = 2 * r0 + cc * GDN_TILE
            wq_ref[0, h, n0:n0 + GDN_TILE, :] = wq.astype(BF16)


def _gdn_prep(qkv, cw, small, alog_row, dtb_row, layer, *, ts):
    b, seq, wide = qkv.shape
    nt = seq // ts
    hd = GDN_HEAD_DIM
    hspec = lambda rows: pl.BlockSpec((1, GDN_HEADS, rows, hd), lambda bi, ti: (bi, 0, ti, 0))
    hshape = lambda rows, dt: jax.ShapeDtypeStruct((b, GDN_HEADS, rows, hd), dt)
    return pl.pallas_call(
        functools.partial(_gdn_prep_body, ts=ts),
        grid=(b, nt),
        in_specs=[
            pl.BlockSpec((1, ts, wide), lambda bi, ti: (bi, ti, 0)),
            pl.BlockSpec((1, 8, wide), lambda bi, ti: (bi, jnp.maximum(ti * (ts // 8) - 1, 0), 0)),
            _layer_spec(cw.shape[1:], layer, lambda bi, ti: (0, 0)),
            pl.BlockSpec((1, ts, LANES), lambda bi, ti: (bi, ti, 0)),
            _layer_spec((1, LANES), layer, lambda bi, ti: (0, 0)),
            _layer_spec((1, LANES), layer, lambda bi, ti: (0, 0)),
        ],
        out_specs=[hspec(ts), hspec(2 * ts), hspec(ts), hspec(ts), hspec(ts // 8)],
        out_shape=[hshape(seq, F32), hshape(2 * seq, BF16), hshape(seq, BF16),
                   hshape(seq, BF16), hshape(seq // 8, F32)],
        compiler_params=_cparams(("parallel", "parallel")),
        name="gdn_prep",
    )(qkv, qkv, cw, small, alog_row, dtb_row)


def _gdn_scan_body(u_ref, wq_ref, kdt_ref, attn_ref, eg_ref, zg_ref, gn_ref, o_ref, st_ref, *, ts):
    @pl.when(pl.program_id(0) == 0)
    def _():
        st_ref[...] = jnp.zeros_like(st_ref)

    hd = GDN_HEAD_DIM
    gn = gn_ref[...]
    zeros = jnp.zeros((GDN_CHUNK, hd), F32)
    chains = [(bi, h) for bi in range(u_ref.shape[0]) for h in range(GDN_HEADS)]
    states = [st_ref[bi, h] for bi, h in chains]
    zsq = jnp.zeros((hd, hd), BF16)

    def block_diag(a, b):
        return jnp.concatenate([jnp.concatenate([a, zsq], axis=1),
                                jnp.concatenate([zsq, b], axis=1)], axis=0)

    for n in range(ts // GDN_CHUNK):
        r0 = n * GDN_CHUNK
        t0 = (n // 2) * GDN_TILE
        for c in range(0, len(chains), 2):
            pair = chains[c:c + 2]
            wq = jnp.concatenate([wq_ref[bi, h, 2 * r0:2 * r0 + GDN_TILE, :] for bi, h in pair], axis=1)
            r = _dot(wq, block_diag(states[c].astype(BF16), states[c + 1].astype(BF16)))
            vpads = []
            for k, (bi, h) in enumerate(pair):
                v_new = u_ref[bi, h, r0:r0 + GDN_CHUNK, :] - r[0:GDN_CHUNK, k * hd:(k + 1) * hd]
                vpad = jnp.concatenate([v_new, zeros] if n % 2 == 0 else [zeros, v_new], axis=0)
                vpads.append(vpad.astype(BF16))
            lhs = jnp.concatenate(
                [jnp.concatenate([kdt_ref[bi, h, t0:t0 + GDN_TILE, :] for bi, h in pair], axis=1),
                 jnp.concatenate([attn_ref[bi, h, r0:r0 + GDN_CHUNK, :] for bi, h in pair], axis=1)], axis=0)
            r2 = _dot(lhs, block_diag(vpads[0], vpads[1]))
            for k, (bi, h) in enumerate(pair):
                sl = slice(k * hd, (k + 1) * hd)
                o = r[GDN_CHUNK:, sl] + r2[GDN_TILE:, sl]
                states[c + k] = states[c + k] * eg_ref[bi, h, 8 * n:8 * n + 1, :] + r2[0:GDN_TILE, sl]
                on = o * lax.rsqrt(jnp.mean(o * o, axis=-1, keepdims=True) + EPS) * gn
                gate = _silu(zg_ref[bi, r0:r0 + GDN_CHUNK, h * hd:(h + 1) * hd])
                o_ref[bi, r0:r0 + GDN_CHUNK, h * hd:(h + 1) * hd] = (on * gate).astype(BF16)
    for c, (bi, h) in enumerate(chains):
        st_ref[bi, h] = states[c]


def _gdn_scan(u, wq, kdt, attn, eg, zg, gn, layer, *, ts):
    b, nh, seq, hd = u.shape
    hspec = lambda rows: pl.BlockSpec((b, nh, rows, hd), lambda ti: (0, 0, ti, 0))
    return pl.pallas_call(
        functools.partial(_gdn_scan_body, ts=ts),
        grid=(seq // ts,),
        in_specs=[hspec(ts), hspec(2 * ts), hspec(ts), hspec(ts), hspec(ts // 8),
                  pl.BlockSpec((b, ts, nh * hd), lambda ti: (0, ti, 0)),
                  _layer_spec((1, hd), layer, lambda ti: (0, 0))],
        out_specs=pl.BlockSpec((b, ts, nh * hd), lambda ti: (0, ti, 0)),
        out_shape=jax.ShapeDtypeStruct((b, seq, nh * hd), BF16),
        scratch_shapes=[pltpu.VMEM((b, nh, hd, hd), F32)],
        compiler_params=_cparams(("arbitrary",)),
        name="gdn_scan",
    )(u, wq, kdt, attn, eg, zg, gn)


def _outproj_body(x_ref, a_ref, b_ref, w_ref, o_ref):
    half = a_ref.shape[1]
    o_ref[...] = x_ref[...] + _dot(a_ref[...], w_ref[0:half, :]) + _dot(b_ref[...], w_ref[half:, :])


def _outproj(x, o_nsa, o_gdn, w_out, layer, *, tm):
    t, d = x.shape
    row = lambda i: (i, 0)
    return pl.pallas_call(
        _outproj_body,
        grid=(t // tm,),
        in_specs=[pl.BlockSpec((tm, d), row), pl.BlockSpec((tm, o_nsa.shape[1]), row),
                  pl.BlockSpec((tm, o_gdn.shape[1]), row),
                  _layer_spec(w_out.shape[1:], layer, lambda i: (0, 0))],
        out_specs=pl.BlockSpec((tm, d), row),
        out_shape=jax.ShapeDtypeStruct((t, d), F32),
        compiler_params=_cparams(("parallel",)),
        name="outproj",
    )(x, o_nsa, o_gdn, w_out)


def _ple_body(x_ref, p_ref, nw_ref, wg_ref, wp_ref, fn_ref, o_ref, *, final):
    x = x_ref[...]
    h = _rms(x, nw_ref[...]).astype(BF16)
    gate = _sigmoid(_dot(h, wg_ref[...]))
    out = x + gate * _dot(p_ref[...].astype(BF16), wp_ref[...])
    if final:
        out = _rms(out, fn_ref[...])
    o_ref[...] = out


def _ple(x, p, nw, wg, wp, fn, layer, *, tm, final):
    t, d = x.shape
    row = lambda i: (i, 0)
    return pl.pallas_call(
        functools.partial(_ple_body, final=final),
        grid=(t // tm,),
        in_specs=[pl.BlockSpec((tm, d), row), _layer_spec((tm, p.shape[2]), layer, row),
                  _layer_spec((1, d), layer, lambda i: (0, 0)),
                  _layer_spec(wg.shape[1:], layer, lambda i: (0, 0)),
                  _layer_spec(wp.shape[1:], layer, lambda i: (0, 0)),
                  pl.BlockSpec(fn.shape, lambda i: (0, 0))],
        out_specs=pl.BlockSpec((tm, d), row),
        out_shape=jax.ShapeDtypeStruct((t, d), F32),
        compiler_params=_cparams(("parallel",)),
        name="ple",
    )(x, p, nw, wg, wp, fn)


def _pack_w_in(w_in):
    offs = [0]
    for s in IN_SIZES:
        offs.append(offs[-1] + s)
    main = w_in[..., offs[0]:offs[7]]
    gates = w_in[..., offs[7]:offs[8]]
    gdn = w_in[..., offs[8]:offs[10]]
    ab = w_in[..., offs[10]:offs[12]]
    per_head = 3 * NSA_GROUP
    zpad = lambda n: jnp.zeros(w_in.shape[:-1] + (n,), w_in.dtype)
    gate_cols = []
    for hk in range(NSA_KV_HEADS):
        gate_cols += [gates[..., hk * per_head:(hk + 1) * per_head], zpad(GATE_ROWS - per_head)]
    tail = zpad(LANES - SMALL_GATE - ab.shape[-1])
    return jnp.concatenate([main, gdn] + gate_cols + [ab, tail], axis=-1).astype(BF16)


def _pack_cmp_w1(w1):
    depth = w1.shape[0]
    w1r = w1.reshape(depth, CMP_BLOCK, NSA_HEAD_DIM, CMP_HIDDEN).astype(BF16)
    z = jnp.zeros_like(w1r)
    return jnp.stack([jnp.concatenate([w1r, z], axis=2), jnp.concatenate([z, w1r], axis=2)], axis=1)


def _rope_tables(seq):
    dim = NSA_HEAD_DIM
    inv = 1.0 / (ROPE_THETA ** (jnp.arange(0, dim, 2, dtype=F32) / dim))
    ang = jnp.arange(seq, dtype=F32)[:, None] * inv[None, :]
    ang = jnp.concatenate([ang, ang], axis=-1)
    cos, sin = jnp.cos(ang), jnp.sin(ang)
    sign = jnp.where(jnp.arange(dim) < dim // 2, -1.0, 1.0).astype(F32)
    return jnp.tile(cos, (1, LANES // dim)), jnp.tile(sin * sign[None, :], (1, LANES // dim))


def _overlap_t(seq, nc_pad):
    n_slc = seq // SLC_BLOCK
    jc = jnp.arange(nc_pad)[None, :]
    js = jnp.arange(LANES)[:, None]
    ov = ((jc * CMP_STRIDE < (js + 1) * SLC_BLOCK) & (jc * CMP_STRIDE + CMP_BLOCK > js * SLC_BLOCK)
          & (js < n_slc))
    return ov.astype(BF16)


def _mask_bands(seq):
    q = (jnp.arange(NSA_GROUP * Q_BLOCK) % Q_BLOCK)[None, :]
    nc = seq // CMP_STRIDE
    rel = jnp.arange(2 * nc)[:, None] - nc
    cband = jnp.where(rel * CMP_STRIDE + (CMP_BLOCK - 1) <= q, 0.0, NEG_INF).astype(F32)
    r = jnp.arange(2 * WINDOW + Q_BLOCK)[:, None]
    wband = jnp.where((q < r) & (r <= q + WINDOW), 0.0, NEG_INF).astype(F32)
    return cband, wband


def _layer(x2, layer, w, consts, *, b, seq, final, cfg):
    t = b * seq
    x2 = _ffn(x2, w['ffn1_norm'], w['ffn1_w1'], w['ffn1_w3'], w['ffn1_w2'], layer,
              tm=cfg['ffn_tm'], tf=cfg['ffn_tf'])
    (q_t, kaug, vs_t, kw, vw_t, gates_t, cmpk, cmpv, qkv, zg, small) = _inproj(
        x2, w['mix_norm'], w['w_in'], consts['cos'], consts['sin'], layer, tm=cfg['in_tm'], b=b, seq=seq)
    kcmp, vcmp_t = _compress(cmpk.reshape(b, seq, LANES), cmpv.reshape(b, seq, LANES), w['cmp_pe'],
                             w['cmp_w1'], w['cmp_w2'], layer)
    o_nsa = _nsa(q_t, kcmp, vcmp_t, kaug, vs_t, kw, vw_t, gates_t, consts['ovt'], consts['cband'],
                 consts['wband'])
    u, wq, kdt, attn, eg = _gdn_prep(qkv.reshape(b, seq, 3 * GDN_WIDTH), w['gdn_conv'],
                                     small.reshape(b, seq, LANES), w['gdn_a_log'], w['gdn_dt_bias'],
                                     layer, ts=cfg['prep_ts'])
    o_gdn = _gdn_scan(u, wq, kdt, attn, eg, zg.reshape(b, seq, GDN_WIDTH), w['gdn_norm'], layer,
                      ts=cfg['scan_ts'])
    x2 = _outproj(x2, o_nsa.reshape(t, NSA_WIDTH), o_gdn.reshape(t, GDN_WIDTH), w['w_out'], layer,
                  tm=cfg['out_tm'])
    x2 = _ffn(x2, w['ffn2_norm'], w['ffn2_w1'], w['ffn2_w3'], w['ffn2_w2'], layer,
              tm=cfg['ffn_tm'], tf=cfg['ffn_tf'])
    return _ple(x2, w['p'], w['ple_norm'], w['ple_gate'], w['ple_proj'], consts['final_norm'], layer,
                tm=cfg['ple_tm'], final=final)


DEFAULT_CFG = dict(ffn_tm=1024, ffn_tf=1408, in_tm=512, prep_ts=256, scan_ts=256,
                   out_tm=512, ple_tm=512)


def _forward(x, p, w, cfg):
    b, seq, d = x.shape
    depth = p.shape[0]
    t = b * seq
    cos2, sin2 = _rope_tables(seq)
    cband, wband = _mask_bands(seq)
    consts = dict(cos=cos2, sin=sin2, ovt=_overlap_t(seq, seq // CMP_STRIDE), cband=cband, wband=wband,
                  final_norm=w['final_norm'].reshape(1, d))
    bf = lambda a: a.astype(BF16)
    row3 = lambda a: a.reshape(depth, 1, a.shape[-1])
    lane_rows = lambda v, off: jnp.zeros((depth, 1, LANES), F32).at[:, 0, off:off + v.shape[1]].set(v)
    pe = jnp.stack([w['cmp_pe_k'], w['cmp_pe_v']], axis=1)
    w2 = jnp.stack([w['cmp_k_w2'], w['cmp_v_w2']], axis=1)
    ws = dict(
        p=p.reshape(depth, t, p.shape[-1]),
        ffn1_norm=row3(w['ffn1_norm']), ffn1_w1=bf(w['ffn1_w1']), ffn1_w3=bf(w['ffn1_w3']),
        ffn1_w2=bf(w['ffn1_w2']),
        mix_norm=row3(w['mix_norm']), w_in=_pack_w_in(w['w_in']),
        cmp_pe=jnp.concatenate([pe, pe], axis=-1),
        cmp_w1=jnp.stack([_pack_cmp_w1(w['cmp_k_w1']), _pack_cmp_w1(w['cmp_v_w1'])], axis=1),
        cmp_w2=bf(jnp.pad(w2, ((0, 0), (0, 0), (0, 0), (0, LANES - w2.shape[-1])))),
        gdn_conv=w['gdn_conv'], gdn_a_log=lane_rows(w['gdn_a_log'], SMALL_A),
        gdn_dt_bias=lane_rows(w['gdn_dt_bias'], SMALL_A), gdn_norm=row3(w['gdn_norm']),
        w_out=bf(w['w_out']),
        ffn2_norm=row3(w['ffn2_norm']), ffn2_w1=bf(w['ffn2_w1']), ffn2_w3=bf(w['ffn2_w3']),
        ffn2_w2=bf(w['ffn2_w2']),
        ple_norm=row3(w['ple_norm']), ple_gate=bf(w['ple_gate']), ple_proj=bf(w['ple_proj']),
    )
    x2 = x.reshape(t, d)
    for i in range(depth):
        x2 = _layer(x2, i, ws, consts, b=b, seq=seq, final=(i == depth - 1), cfg=cfg)
    return x2.reshape(b, seq, d)


def kernel(x, p, ffn1_norm, ffn1_w1, ffn1_w3, ffn1_w2, mix_norm, w_in, cmp_pe_k, cmp_pe_v,
           cmp_k_w1, cmp_k_w2, cmp_v_w1, cmp_v_w2, gdn_conv, gdn_a_log, gdn_dt_bias, gdn_norm,
           w_out, ffn2_norm, ffn2_w1, ffn2_w3, ffn2_w2, ple_norm, ple_gate, ple_proj, final_norm):
    w = dict(ffn1_norm=ffn1_norm, ffn1_w1=ffn1_w1, ffn1_w3=ffn1_w3, ffn1_w2=ffn1_w2,
             mix_norm=mix_norm, w_in=w_in, cmp_pe_k=cmp_pe_k, cmp_pe_v=cmp_pe_v,
             cmp_k_w1=cmp_k_w1, cmp_k_w2=cmp_k_w2, cmp_v_w1=cmp_v_w1, cmp_v_w2=cmp_v_w2,
             gdn_conv=gdn_conv, gdn_a_log=gdn_a_log, gdn_dt_bias=gdn_dt_bias, gdn_norm=gdn_norm,
             w_out=w_out, ffn2_norm=ffn2_norm, ffn2_w1=ffn2_w1, ffn2_w3=ffn2_w3, ffn2_w2=ffn2_w2,
             ple_norm=ple_norm, ple_gate=ple_gate, ple_proj=ple_proj, final_norm=final_norm)
    return _forward(x, p, w, DEFAULT_CFG)
```

```python
import functools

import jax
import jax.numpy as jnp
from jax import lax
from jax.experimental import pallas as pl
from jax.experimental.pallas import tpu as pltpu

F32 = jnp.float32
BF16 = jnp.bfloat16

D_MODEL = 1024
NSA_HEADS = 8
NSA_KV_HEADS = 2
NSA_GROUP = NSA_HEADS // NSA_KV_HEADS
NSA_HEAD_DIM = 64
CMP_BLOCK = 32
CMP_STRIDE = 16
CMP_HIDDEN = 128
SLC_BLOCK = 64
N_SELECTED = 16
WINDOW = 512
Q_BLOCK = 128
GDN_HEADS = 4
GDN_HEAD_DIM = 128
GDN_CHUNK = 64
CONV_WIDTH = 4
D_FF = 2816
PLE_DIM = 256
ROPE_THETA = 10000.0
EPS = 1e-6
FORCE_SCORE = 1e6
NEG_INF = -1e30

NSA_WIDTH = NSA_HEADS * NSA_HEAD_DIM
NSA_KV_WIDTH = NSA_KV_HEADS * NSA_HEAD_DIM
GDN_WIDTH = GDN_HEADS * GDN_HEAD_DIM
IN_SIZES = (NSA_WIDTH, NSA_KV_WIDTH, NSA_KV_WIDTH, NSA_KV_WIDTH, NSA_KV_WIDTH,
            NSA_KV_WIDTH, NSA_KV_WIDTH, 3 * NSA_HEADS, 3 * GDN_WIDTH, GDN_WIDTH,
            GDN_HEADS, GDN_HEADS)

LANES = 128
NSA_MAIN = NSA_WIDTH + 6 * NSA_KV_WIDTH
GDN_MAIN = 4 * GDN_WIDTH
GATE_ROWS = 16
SMALL_GATE = NSA_KV_HEADS * GATE_ROWS
SMALL_A = SMALL_GATE
SMALL_B = SMALL_GATE + GDN_HEADS
W_IN_PACKED = NSA_MAIN + GDN_MAIN + LANES
V_ROWS = NSA_HEAD_DIM + 16
SEL_BIAS = -2.0 ** 100
M_INIT = -3.0e38
LOG2_E = 1.4426950408889634
VMEM_LIMIT = 56 * 1024 * 1024


def _cparams(sem):
    return pltpu.CompilerParams(dimension_semantics=sem, vmem_limit_bytes=VMEM_LIMIT)


def _rms(x, w):
    ms = jnp.mean(x * x, axis=-1, keepdims=True)
    return x * lax.rsqrt(ms + EPS) * w


def _sigmoid(x):
    return 1.0 / (1.0 + jnp.exp(-x))


def _silu(x):
    return x * _sigmoid(x)


def _dot(a, b):
    return jnp.dot(a, b, preferred_element_type=F32)


def _dot_nt(a, b):
    return lax.dot_general(a, b, (((1,), (1,)), ((), ())), preferred_element_type=F32)


def _split3(x):
    hi = x.astype(BF16)
    r = x - hi.astype(F32)
    mid = r.astype(BF16)
    lo = (r - mid.astype(F32)).astype(BF16)
    return hi, mid, lo


def _dot_exact_lhs(a_bf, x):
    hi, mid, lo = _split3(x)
    return _dot(a_bf, hi) + (_dot(a_bf, mid) + _dot(a_bf, lo))


def _split2(x):
    hi = x.astype(BF16)
    return hi, (x - hi.astype(F32)).astype(BF16)


def _dot_split(a, b):
    ah, am = a
    bh, bm = b
    return _dot(ah, bh) + (_dot(ah, bm) + _dot(am, bh))


def _layer_spec(block, layer, index):
    return pl.BlockSpec((None,) + block, lambda *g: (layer,) + index(*g))


def _ffn_body(x_ref, nw_ref, w1_ref, w3_ref, w2_ref, *rest, mixed):
    if mixed:
        a_ref, b_ref, wo_ref, o_ref, h_ref, xs_ref = rest
    else:
        o_ref, h_ref = rest
        xs_ref = x_ref
    j = pl.program_id(1)

    @pl.when(j == 0)
    def _():
        if mixed:
            half = a_ref.shape[1]
            xs_ref[...] = (x_ref[...] + _dot(a_ref[...], wo_ref[0:half, :])
                           + _dot(b_ref[...], wo_ref[half:, :]))
        h_ref[...] = _rms(xs_ref[...], nw_ref[...]).astype(BF16)
        o_ref[...] = jnp.zeros_like(o_ref)

    h = h_ref[...]
    u = _dot(h, w1_ref[...])
    g = _dot(h, w3_ref[...])
    a = (_silu(u) * g).astype(BF16)
    o_ref[...] += _dot(a, w2_ref[...])

    @pl.when(j == pl.num_programs(1) - 1)
    def _():
        o_ref[...] = xs_ref[...] + 0.5 * o_ref[...]


def _ffn(x, nw, w1, w3, w2, layer, *, tm, tf, mix=None):
    t, d = x.shape
    ff = w1.shape[2]
    row = lambda i, j: (i, 0)
    in_specs = [
        pl.BlockSpec((tm, d), row),
        _layer_spec((1, d), layer, lambda i, j: (0, 0)),
        _layer_spec((d, tf), layer, lambda i, j: (0, j)),
        _layer_spec((d, tf), layer, lambda i, j: (0, j)),
        _layer_spec((tf, d), layer, lambda i, j: (j, 0)),
    ]
    scratch = [pltpu.VMEM((tm, d), BF16)]
    args = (x, nw, w1, w3, w2)
    if mix is not None:
        a, b, w_out = mix
        in_specs += [pl.BlockSpec((tm, a.shape[1]), row), pl.BlockSpec((tm, b.shape[1]), row),
                     _layer_spec(w_out.shape[1:], layer, lambda i, j: (0, 0))]
        scratch.append(pltpu.VMEM((tm, d), F32))
        args += (a, b, w_out)
    return pl.pallas_call(
        functools.partial(_ffn_body, mixed=mix is not None),
        grid=(t // tm, ff // tf),
        in_specs=in_specs,
        out_specs=pl.BlockSpec((tm, d), row),
        out_shape=jax.ShapeDtypeStruct((t, d), F32),
        scratch_shapes=scratch,
        compiler_params=_cparams(("parallel", "arbitrary")),
        name="ffn_mix" if mix is not None else "ffn",
    )(*args)


def _rope(xg, cos, sin_signed, first_half):
    fwd = pltpu.roll(xg, LANES - NSA_HEAD_DIM // 2, 1)
    bwd = pltpu.roll(xg, NSA_HEAD_DIM // 2, 1)
    return xg * cos + jnp.where(first_half, fwd, bwd) * sin_signed


def _inproj_body(x_ref, nw_ref, w_ref, cos_ref, sin_ref,
                 qt_ref, kaug_ref, vst_ref, kw_ref, vwt_ref, gt_ref,
                 cmpk_ref, cmpv_ref, qkv_ref, zg_ref, small_ref, *, nseq):
    tm = x_ref.shape[0]
    nq = tm // Q_BLOCK
    dh = NSA_HEAD_DIM
    h = _rms(x_ref[...], nw_ref[...]).astype(BF16)
    cos = cos_ref[...]
    sin_s = sin_ref[...]
    lane = lax.broadcasted_iota(jnp.int32, (1, LANES), 1)
    first_half = (lane & (dh - 1)) < (dh // 2)
    low = lane < dh

    z = _dot(h, w_ref[:, 0:NSA_MAIN])
    scale = dh ** -0.5 * LOG2_E
    for pair in range(NSA_WIDTH // LANES):
        zq = _rope(z[:, pair * LANES:(pair + 1) * LANES], cos, sin_s, first_half) * scale
        tr = jnp.transpose(zq).astype(BF16)
        for half in range(2):
            hk, g = divmod(2 * pair + half, NSA_GROUP)
            for qb in range(nq):
                qt_ref[0, hk, qb, 0:dh, g * Q_BLOCK:(g + 1) * Q_BLOCK] = (
                    tr[half * dh:(half + 1) * dh, qb * Q_BLOCK:(qb + 1) * Q_BLOCK])
    qt_ref[0, :, :, dh:, :] = jnp.zeros((NSA_KV_HEADS, nq, LANES - dh, NSA_GROUP * Q_BLOCK), BF16)

    def group(c, rotary):
        zc = z[:, NSA_WIDTH + c * LANES: NSA_WIDTH + (c + 1) * LANES]
        return _rope(zc, cos, sin_s, first_half) if rotary else zc

    def heads(zc):
        return [jnp.where(low, zc, 0.0), jnp.where(low, pltpu.roll(zc, dh, 1), 0.0)]

    cmpk_ref[...] = group(0, True)
    cmpv_ref[...] = group(1, False)
    tok = (pl.program_id(0) % nseq) * tm + lax.broadcasted_iota(jnp.int32, (tm, 1), 0)
    local_blk = jnp.right_shift(tok, SLC_BLOCK.bit_length() - 1) & (tm // SLC_BLOCK - 1)
    onehot = jnp.where(lane - dh == local_blk, 1.0, 0.0)
    for hk, kh in enumerate(heads(group(2, True))):
        kaug_ref[0, hk] = jnp.where(low, kh, onehot).astype(BF16)
    for hk, kh in enumerate(heads(group(4, True))):
        kw_ref[0, hk] = kh.astype(BF16)
    vst = jnp.transpose(group(3, False)).astype(BF16)
    ones_rows = jnp.where(lax.broadcasted_iota(jnp.int32, (V_ROWS - dh, tm), 0) == 0, 1.0, 0.0)
    vwt = jnp.transpose(group(5, False)).astype(BF16)
    for hk in range(NSA_KV_HEADS):
        vst_ref[0, hk, 0] = jnp.concatenate([vst[hk * dh:(hk + 1) * dh], ones_rows.astype(BF16)], axis=0)
        for qb in range(nq):
            vwt_ref[0, hk, qb] = vwt[hk * dh:(hk + 1) * dh, qb * Q_BLOCK:(qb + 1) * Q_BLOCK]

    zg = _dot(h, w_ref[:, NSA_MAIN:NSA_MAIN + GDN_MAIN])
    qkv_ref[...] = zg[:, 0:3 * GDN_WIDTH]
    zg_ref[...] = zg[:, 3 * GDN_WIDTH:]

    zs = _dot(h, w_ref[:, NSA_MAIN + GDN_MAIN:])
    is_raw = (lane >= SMALL_A) & (lane < SMALL_B)
    small = jnp.where(is_raw, zs, _sigmoid(zs))
    small_ref[...] = small
    small_t = jnp.transpose(small)
    for qb in range(nq):
        gt_ref[0, qb] = small_t[0:SMALL_GATE, qb * Q_BLOCK:(qb + 1) * Q_BLOCK]


def _inproj(x, nw, w_packed, cos2, sin2, layer, *, tm, b, seq):
    t, d = x.shape
    nseq = seq // tm
    nq = tm // Q_BLOCK
    nqb = seq // Q_BLOCK
    hkv, dh = NSA_KV_HEADS, NSA_HEAD_DIM
    cols = NSA_GROUP * Q_BLOCK
    row = lambda i: (i, 0)
    tile5 = lambda i: (i // nseq, 0, i % nseq, 0, 0)
    tile4 = lambda i: (i // nseq, 0, i % nseq, 0)
    return pl.pallas_call(
        functools.partial(_inproj_body, nseq=nseq),
        grid=(t // tm,),
        in_specs=[
            pl.BlockSpec((tm, d), row),
            _layer_spec((1, d), layer, lambda i: (0, 0)),
            _layer_spec((d, W_IN_PACKED), layer, lambda i: (0, 0)),
            pl.BlockSpec((tm, LANES), lambda i: (i % nseq, 0)),
            pl.BlockSpec((tm, LANES), lambda i: (i % nseq, 0)),
        ],
        out_specs=[
            pl.BlockSpec((1, hkv, nq, LANES, cols), tile5),
            pl.BlockSpec((1, hkv, tm, LANES), tile4),
            pl.BlockSpec((1, hkv, 1, V_ROWS, tm), tile5),
            pl.BlockSpec((1, hkv, tm, LANES), tile4),
            pl.BlockSpec((1, hkv, nq, dh, Q_BLOCK), tile5),
            pl.BlockSpec((1, nq, SMALL_GATE, Q_BLOCK), lambda i: (i // nseq, i % nseq, 0, 0)),
            pl.BlockSpec((tm, NSA_KV_WIDTH), row),
            pl.BlockSpec((tm, NSA_KV_WIDTH), row),
            pl.BlockSpec((tm, 3 * GDN_WIDTH), row),
            pl.BlockSpec((tm, GDN_WIDTH), row),
            pl.BlockSpec((tm, LANES), row),
        ],
        out_shape=[
            jax.ShapeDtypeStruct((b, hkv, nqb, LANES, cols), BF16),
            jax.ShapeDtypeStruct((b, hkv, seq, LANES), BF16),
            jax.ShapeDtypeStruct((b, hkv, nseq, V_ROWS, tm), BF16),
            jax.ShapeDtypeStruct((b, hkv, seq, LANES), BF16),
            jax.ShapeDtypeStruct((b, hkv, nqb, dh, Q_BLOCK), BF16),
            jax.ShapeDtypeStruct((b, nqb, SMALL_GATE, Q_BLOCK), F32),
            jax.ShapeDtypeStruct((t, NSA_KV_WIDTH), F32),
            jax.ShapeDtypeStruct((t, NSA_KV_WIDTH), F32),
            jax.ShapeDtypeStruct((t, 3 * GDN_WIDTH), F32),
            jax.ShapeDtypeStruct((t, GDN_WIDTH), F32),
            jax.ShapeDtypeStruct((t, LANES), F32),
        ],
        compiler_params=_cparams(("parallel",)),
        name="inproj",
    )(x, nw, w_packed, cos2, sin2)


def _compress_body(xk_ref, xv_ref, pe_ref, w1_ref, w2_ref, ok_ref, ov_ref):
    nh = ok_ref.shape[2]
    hkv = NSA_KV_HEADS
    lo = [[jnp.zeros((nh, CMP_HIDDEN), F32) for _ in range(hkv)] for _ in range(2)]
    hi = [[jnp.zeros((nh, CMP_HIDDEN), F32) for _ in range(hkv)] for _ in range(2)]
    for l in range(CMP_STRIDE):
        for kv, x_ref in enumerate((xk_ref, xv_ref)):
            xg = x_ref[0, pl.ds(l, nh, stride=CMP_STRIDE), :]
            x_lo = (xg + pe_ref[kv, l:l + 1, :]).astype(BF16)
            x_hi = (xg + pe_ref[kv, CMP_STRIDE + l:CMP_STRIDE + l + 1, :]).astype(BF16)
            for hk in range(hkv):
                lo[kv][hk] = lo[kv][hk] + _dot(x_lo, w1_ref[kv, hk, l])
                hi[kv][hk] = hi[kv][hk] + _dot(x_hi, w1_ref[kv, hk, CMP_STRIDE + l])
    for hk in range(hkv):
        hid_k = lo[0][hk] + pltpu.roll(hi[0][hk], nh - 1, 0)
        hid_v = lo[1][hk] + pltpu.roll(hi[1][hk], nh - 1, 0)
        ok_ref[0, hk] = _dot(_silu(hid_k).astype(BF16), w2_ref[0]).astype(BF16)
        vt = jnp.transpose(_dot(_silu(hid_v).astype(BF16), w2_ref[1]))
        ov_ref[0, hk] = vt[0:NSA_HEAD_DIM].astype(BF16)


def _compress(xk, xv, pe, w1, w2, layer):
    b, seq, wide = xk.shape
    nh = seq // CMP_STRIDE
    hkv = NSA_KV_HEADS
    return pl.pallas_call(
        _compress_body,
        grid=(b,),
        in_specs=[pl.BlockSpec((1, seq, wide), lambda i: (i, 0, 0)),
                  pl.BlockSpec((1, seq, wide), lambda i: (i, 0, 0)),
                  _layer_spec(pe.shape[1:], layer, lambda i: (0, 0, 0)),
                  _layer_spec(w1.shape[1:], layer, lambda i: (0, 0, 0, 0, 0)),
                  _layer_spec(w2.shape[1:], layer, lambda i: (0, 0, 0))],
        out_specs=[pl.BlockSpec((1, hkv, nh, LANES), lambda i: (i, 0, 0, 0)),
                   pl.BlockSpec((1, hkv, NSA_HEAD_DIM, nh), lambda i: (i, 0, 0, 0))],
        out_shape=[jax.ShapeDtypeStruct((b, hkv, nh, LANES), BF16),
                   jax.ShapeDtypeStruct((b, hkv, NSA_HEAD_DIM, nh), BF16)],
        compiler_params=_cparams(("parallel",)),
        name="compress",
    )(xk, xv, pe, w1, w2)


def _colmax(s):
    r = s.shape[0]
    while r > 8 and r % 4 == 0:
        r //= 4
        s = jnp.max(s.reshape(4, r, s.shape[-1]), axis=0)
    return jnp.max(s, axis=0, keepdims=True)


def _nsa_body(q_ref, kc_ref, vc_ref, kaug_ref, vs_ref, kw_ref, vw_ref, g_ref, ovt_ref, cband_ref,
              wband_ref, o_ref, s_ref, b_ref, *, n_slc, n_sel, tk):
    i = pl.program_id(1)
    t0 = i * Q_BLOCK
    cols = NSA_GROUP * Q_BLOCK
    dh = NSA_HEAD_DIM
    heads = range(NSA_KV_HEADS)
    q_t = [q_ref[0, h, 0] for h in heads]
    lane = lax.broadcasted_iota(jnp.int32, (1, cols), 1)
    tq = t0 + (lane & (Q_BLOCK - 1))

    def softmax_cols(s, zero=None):
        m = _colmax(s)
        if zero is not None:
            m = m + zero
        e = jnp.exp2(s - m)
        return e, jnp.sum(e, axis=0, keepdims=True)

    nc = kc_ref.shape[2]
    per_q = Q_BLOCK // CMP_STRIDE
    cband = cband_ref[pl.ds(pl.multiple_of(nc - per_q * i, per_q), nc), :]
    wspan = Q_BLOCK + WINDOW
    c0 = jnp.maximum(i - WINDOW // Q_BLOCK, 0)
    start = pl.multiple_of(c0 * Q_BLOCK, Q_BLOCK)
    shift = pl.multiple_of(jnp.maximum(WINDOW - t0, 0), Q_BLOCK)
    wband = wband_ref[pl.ds(shift, wspan), :]
    o_cmp, o_win, imps = [], [], []
    for h in heads:
        e_c, l_c = softmax_cols(_dot(kc_ref[0, h], q_t[h]) + cband)
        inv_c = jnp.where(tq >= CMP_BLOCK - 1, 1.0 / l_c, 0.0)
        o_cmp.append(_dot(vc_ref[0, h], e_c.astype(BF16)) * inv_c)
        p_c = e_c * inv_c
        pg = p_c[:, 0:Q_BLOCK]
        for g in range(1, NSA_GROUP):
            pg = pg + p_c[:, g * Q_BLOCK:(g + 1) * Q_BLOCK]
        imp = _dot_exact_lhs(ovt_ref[...], pg)
        imps.append(imp)
        zero = jnp.concatenate([jnp.where(imp[0:1, :] > 1e30, 1.0, 0.0)] * NSA_GROUP, axis=1)
        e_w, l_w = softmax_cols(_dot(kw_ref[0, h, pl.ds(start, wspan), :], q_t[h]) + wband, zero)
        e_w = e_w.astype(BF16)
        ow = _dot(vw_ref[0, h, c0], e_w[0:Q_BLOCK])
        for c in range(1, wspan // Q_BLOCK):
            ow = ow + _dot(vw_ref[0, h, c0 + c], e_w[c * Q_BLOCK:(c + 1) * Q_BLOCK])
        o_win.append(ow * (1.0 / l_w))

    blk = lax.broadcasted_iota(jnp.int32, (LANES, Q_BLOCK), 0)
    tcol = t0 + lax.broadcasted_iota(jnp.int32, (LANES, Q_BLOCK), 1)
    cur = jnp.right_shift(tcol, SLC_BLOCK.bit_length() - 1)
    forced = (blk == 0) | (blk == cur) | (blk == cur - 1)
    valid = blk * SLC_BLOCK <= tcol
    score = [jnp.where((blk < n_slc) & jnp.logical_not(forced), jnp.where(valid, imps[h], -1.0), -jnp.inf)
             for h in heads]
    bias = [jnp.where(forced, 0.0, SEL_BIAS) for _ in heads]
    for _ in range(max(n_sel - 3, 0)):
        for h in heads:
            mx = jnp.max(score[h], axis=0, keepdims=True)
            idx = jnp.min(jnp.where(score[h] == mx, blk, 2 * LANES), axis=0, keepdims=True)
            hit = blk == idx
            bias[h] = jnp.where(hit, 0.0, bias[h])
            score[h] = jnp.where(hit, -jnp.inf, score[h])
    for h in heads:
        b_ref[h] = jnp.concatenate([bias[h]] * NSA_GROUP, axis=1)

    nb = tk // SLC_BLOCK
    pad = jnp.zeros((16 - nb, cols), F32)

    def scores(h, j):
        rows = jnp.concatenate([b_ref[h, pl.ds(pl.multiple_of(j * nb, nb), nb), :], pad], axis=0)
        q_aug = jnp.concatenate([q_t[h][0:dh], rows.astype(BF16), q_t[h][dh + 16:]], axis=0)
        return _dot(kaug_ref[0, h, pl.ds(pl.multiple_of(j * tk, tk), tk), :], q_aug)

    def absorb(h, j, slot, m, acc, causal):
        s = s_ref[h, slot]
        if causal:
            kpos = j * tk + lax.broadcasted_iota(jnp.int32, (tk, 1), 0)
            s = jnp.where(kpos <= tq, s, NEG_INF)
        m_new = jnp.maximum(m, _colmax(s))
        p = jnp.exp2(s - m_new).astype(BF16)
        return m_new, jnp.exp2(m - m_new) * acc + _dot(vs_ref[0, h, j], p)

    def absorb_all(j, slot, carry, causal):
        out = ()
        for h in heads:
            out += absorb(h, j, slot, carry[2 * h], carry[2 * h + 1], causal)
        return out

    def fill(slot, j):
        for h in heads:
            s_ref[h, slot] = scores(h, j)

    def slc_pair(jj, carry):
        j = 2 * jj
        fill(1, j + 1)
        carry = absorb_all(j, 0, carry, False)
        fill(0, j + 2)
        return absorb_all(j + 1, 1, carry, False)

    def tail_odd(*carry):
        fill(1, n_full)
        carry = absorb_all(n_full - 1, 0, carry, False)
        return absorb_all(n_full, 1, carry, True)

    def tail_even(*carry):
        return absorb_all(n_full, 0, carry, True)

    n_full = t0 // tk
    fill(0, 0)
    init = (jnp.full((1, cols), M_INIT, F32), jnp.zeros((V_ROWS, cols), F32)) * NSA_KV_HEADS
    carry = lax.fori_loop(0, n_full // 2, slc_pair, init)
    carry = lax.cond(n_full % 2 == 1, tail_odd, tail_even, *carry)

    pairs = []
    for h in heads:
        acc_s = carry[2 * h + 1]
        o_slc = acc_s[0:dh] * (1.0 / acc_s[dh:dh + 1])
        gate = g_ref[0, 0, h * GATE_ROWS:(h + 1) * GATE_ROWS]
        outs = []
        for g in range(NSA_GROUP):
            sl = slice(g * Q_BLOCK, (g + 1) * Q_BLOCK)
            outs.append(gate[3 * g:3 * g + 1] * o_cmp[h][:, sl] + gate[3 * g + 1:3 * g + 2] * o_slc[:, sl]
                        + gate[3 * g + 2:3 * g + 3] * o_win[h][:, sl])
        pairs += [jnp.transpose(jnp.concatenate(outs[2 * k:2 * k + 2], axis=0))
                  for k in range(NSA_GROUP // 2)]
    o_ref[0] = jnp.concatenate(pairs, axis=1).astype(BF16)


def _nsa(q_t, kcmp, vcmp_t, kaug, vs_t, kw, vw_t, gates_t, ovt, cband, wband):
    b, hkv, nqb, _, cols = q_t.shape
    seq = kaug.shape[2]
    tk = vs_t.shape[4]
    n_slc = seq // SLC_BLOCK
    per_batch = lambda a: pl.BlockSpec((1,) + a.shape[1:], lambda bi, qi: (bi,) + (0,) * (a.ndim - 1))
    const = lambda a: pl.BlockSpec(a.shape, lambda bi, qi: (0, 0))
    body = functools.partial(_nsa_body, n_slc=n_slc, n_sel=min(N_SELECTED, n_slc), tk=tk)
    return pl.pallas_call(
        body,
        grid=(b, nqb),
        in_specs=[pl.BlockSpec((1, hkv, 1) + q_t.shape[3:], lambda bi, qi: (bi, 0, qi, 0, 0)),
                  per_batch(kcmp), per_batch(vcmp_t), per_batch(kaug), per_batch(vs_t),
                  per_batch(kw), per_batch(vw_t),
                  pl.BlockSpec((1, 1, SMALL_GATE, Q_BLOCK), lambda bi, qi: (bi, qi, 0, 0)),
                  const(ovt), const(cband), const(wband)],
        out_specs=pl.BlockSpec((1, Q_BLOCK, NSA_WIDTH), lambda bi, qi: (bi, qi, 0)),
        out_shape=jax.ShapeDtypeStruct((b, seq, NSA_WIDTH), BF16),
        scratch_shapes=[pltpu.VMEM((hkv, 2, tk, cols), F32), pltpu.VMEM((hkv, LANES, cols), F32)],
        compiler_params=_cparams(("parallel", "arbitrary")),
        name="nsa",
    )(q_t, kcmp, vcmp_t, kaug, vs_t, kw, vw_t, gates_t, ovt, cband, wband)


GDN_TILE = 2 * GDN_CHUNK


def _gdn_prep_body(qkv_ref, halo_ref, cw_ref, small_ref, alog_ref, dtb_ref,
                   u_ref, wq_ref, kdt_ref, attn_ref, eg_ref, *, ts):
    i = pl.program_id(1)
    x = qkv_ref[0]
    halo = jnp.where(i > 0, halo_ref[0], 0.0)
    xx = jnp.concatenate([halo, x], axis=0)
    y = x * cw_ref[CONV_WIDTH - 1:CONV_WIDTH, :]
    for d in range(1, CONV_WIDTH):
        shifted = pltpu.roll(xx, d, 0)[8:]
        y = y + shifted * cw_ref[CONV_WIDTH - 1 - d:CONV_WIDTH - d, :]
    y = _silu(y)

    sm = small_ref[0]
    sp_in = sm + dtb_ref[...]
    softplus = jnp.maximum(sp_in, 0.0) + jnp.log(1.0 + jnp.exp(-jnp.abs(sp_in)))
    glog = -jnp.exp(alog_ref[...]) * softplus

    ri = lax.broadcasted_iota(jnp.int32, (ts, ts), 0)
    ci = lax.broadcasted_iota(jnp.int32, (ts, ts), 1)
    sh = GDN_CHUNK.bit_length() - 1
    same = jnp.right_shift(ri, sh) == jnp.right_shift(ci, sh)
    tril = jnp.where(same & (ri >= ci), 1.0, 0.0).astype(BF16)
    ones = jnp.where(same, 1.0, 0.0).astype(BF16)
    gcum = _dot_exact_lhs(tril, glog)
    glast = _dot_exact_lhs(ones, glog)

    r2 = lax.broadcasted_iota(jnp.int32, (GDN_TILE, GDN_TILE), 0)
    c2 = lax.broadcasted_iota(jnp.int32, (GDN_TILE, GDN_TILE), 1)
    same2 = jnp.right_shift(r2, sh) == jnp.right_shift(c2, sh)
    incl = same2 & (r2 >= c2)
    strict = same2 & (r2 > c2)
    eye = jnp.where(r2 == c2, 1.0, 0.0)
    qscale = GDN_HEAD_DIM ** -0.5

    units = []
    for c in range(ts // GDN_TILE):
        r0 = c * GDN_TILE
        gc_tile = gcum[r0:r0 + GDN_TILE]
        gc_rows = jnp.transpose(gc_tile)
        for h in range(GDN_HEADS):
            lo = h * GDN_HEAD_DIM
            qh = y[r0:r0 + GDN_TILE, lo:lo + GDN_HEAD_DIM]
            kh = y[r0:r0 + GDN_TILE, GDN_WIDTH + lo:GDN_WIDTH + lo + GDN_HEAD_DIM]
            vh = y[r0:r0 + GDN_TILE, 2 * GDN_WIDTH + lo:2 * GDN_WIDTH + lo + GDN_HEAD_DIM]
            qh = qh * lax.rsqrt(jnp.sum(qh * qh, axis=-1, keepdims=True) + EPS)
            kh = kh * lax.rsqrt(jnp.sum(kh * kh, axis=-1, keepdims=True) + EPS)
            gc_col = gc_tile[:, SMALL_A + h:SMALL_A + h + 1]
            gc_row = gc_rows[SMALL_A + h:SMALL_A + h + 1, :]
            gl_col = glast[r0:r0 + GDN_TILE, SMALL_A + h:SMALL_A + h + 1]
            beta = sm[r0:r0 + GDN_TILE, SMALL_B + h:SMALL_B + h + 1]

            decay = jnp.where(incl, jnp.exp(jnp.minimum(gc_col - gc_row, 0.0)), 0.0)
            kb = kh * beta
            k_bf = kh.astype(BF16)
            a_s = jnp.where(strict, _dot_nt(kb.astype(BF16), k_bf) * decay, 0.0)
            egc = jnp.exp(gc_col)
            qs = qh * qscale
            attn = jnp.where(incl, _dot_nt(qs.astype(BF16), k_bf) * decay, 0.0)
            attn_ref[0, h, r0:r0 + GDN_TILE, :] = attn.astype(BF16)
            k_dec = kh * jnp.exp(gl_col - gc_col)
            kdt_ref[0, h, r0:r0 + GDN_TILE, :] = jnp.transpose(k_dec).astype(BF16)
            for cc in range(2):
                e0 = (r0 // GDN_CHUNK + cc) * 8
                eg_ref[0, h, e0:e0 + 8, :] = jnp.broadcast_to(
                    jnp.exp(gl_col[cc * GDN_CHUNK:cc * GDN_CHUNK + 8]), (8, GDN_HEAD_DIM))
            units.append(dict(h=h, r0=r0, a=a_s, rhs_u=vh * beta, rhs_w=kb * egc, q_dec=qs * egc))

    xinv = [eye - un['a'] for un in units]
    asp = [_split2(un['a']) for un in units]
    pw = [_dot_split(a, a) for a in asp]
    steps = GDN_CHUNK.bit_length() - 2
    for s in range(steps):
        pws = [_split2(p) for p in pw]
        xinv = [x + _dot_split(_split2(x), p) for x, p in zip(xinv, pws)]
        if s + 1 < steps:
            pw = [_dot_split(p, p) for p in pws]

    for un, x in zip(units, xinv):
        h, r0 = un['h'], un['r0']
        xs = _split2(x)
        u = _dot_split(xs, _split2(un['rhs_u']))
        w = _dot_split(xs, _split2(un['rhs_w']))
        u_ref[0, h, r0:r0 + GDN_TILE, :] = u
        for cc in range(2):
            a0 = cc * GDN_CHUNK
            wq = jnp.concatenate([w[a0:a0 + GDN_CHUNK], un['q_dec'][a0:a0 + GDN_CHUNK]], axis=0)
            n0 = 2 * r0 + cc * GDN_TILE
            wq_ref[0, h, n0:n0 + GDN_TILE, :] = wq.astype(BF16)


def _gdn_prep(qkv, cw, small, alog_row, dtb_row, layer, *, ts):
    b, seq, wide = qkv.shape
    nt = seq // ts
    hd = GDN_HEAD_DIM
    hspec = lambda rows: pl.BlockSpec((1, GDN_HEADS, rows, hd), lambda bi, ti: (bi, 0, ti, 0))
    hshape = lambda rows, dt: jax.ShapeDtypeStruct((b, GDN_HEADS, rows, hd), dt)
    return pl.pallas_call(
        functools.partial(_gdn_prep_body, ts=ts),
        grid=(b, nt),
        in_specs=[
            pl.BlockSpec((1, ts, wide), lambda bi, ti: (bi, ti, 0)),
            pl.BlockSpec((1, 8, wide), lambda bi, ti: (bi, jnp.maximum(ti * (ts // 8) - 1, 0), 0)),
            _layer_spec(cw.shape[1:], layer, lambda bi, ti: (0, 0)),
            pl.BlockSpec((1, ts, LANES), lambda bi, ti: (bi, ti, 0)),
            _layer_spec((1, LANES), layer, lambda bi, ti: (0, 0)),
            _layer_spec((1, LANES), layer, lambda bi, ti: (0, 0)),
        ],
        out_specs=[hspec(ts), hspec(2 * ts), hspec(ts), hspec(ts), hspec(ts // 8)],
        out_shape=[hshape(seq, F32), hshape(2 * seq, BF16), hshape(seq, BF16),
                   hshape(seq, BF16), hshape(seq // 8, F32)],
        compiler_params=_cparams(("parallel", "parallel")),
        name="gdn_prep",
    )(qkv, qkv, cw, small, alog_row, dtb_row)


def _gdn_scan_body(u_ref, wq_ref, kdt_ref, attn_ref, eg_ref, zg_ref, gn_ref, o_ref, st_ref, *, ts):
    @pl.when(pl.program_id(0) == 0)
    def _():
        st_ref[...] = jnp.zeros_like(st_ref)

    hd = GDN_HEAD_DIM
    gn = gn_ref[...]
    zeros = jnp.zeros((GDN_CHUNK, hd), F32)
    chains = [(bi, h) for bi in range(u_ref.shape[0]) for h in range(GDN_HEADS)]
    states = [st_ref[bi, h] for bi, h in chains]
    zsq = jnp.zeros((hd, hd), BF16)

    def block_diag(a, b):
        return jnp.concatenate([jnp.concatenate([a, zsq], axis=1),
                                jnp.concatenate([zsq, b], axis=1)], axis=0)

    for n in range(ts // GDN_CHUNK):
        r0 = n * GDN_CHUNK
        t0 = (n // 2) * GDN_TILE
        for c in range(0, len(chains), 2):
            pair = chains[c:c + 2]
            wq = jnp.concatenate([wq_ref[bi, h, 2 * r0:2 * r0 + GDN_TILE, :] for bi, h in pair], axis=1)
            r = _dot(wq, block_diag(states[c].astype(BF16), states[c + 1].astype(BF16)))
            vpads = []
            for k, (bi, h) in enumerate(pair):
                v_new = u_ref[bi, h, r0:r0 + GDN_CHUNK, :] - r[0:GDN_CHUNK, k * hd:(k + 1) * hd]
                vpad = jnp.concatenate([v_new, zeros] if n % 2 == 0 else [zeros, v_new], axis=0)
                vpads.append(vpad.astype(BF16))
            lhs = jnp.concatenate(
                [jnp.concatenate([kdt_ref[bi, h, t0:t0 + GDN_TILE, :] for bi, h in pair], axis=1),
                 jnp.concatenate([attn_ref[bi, h, r0:r0 + GDN_CHUNK, :] for bi, h in pair], axis=1)], axis=0)
            r2 = _dot(lhs, block_diag(vpads[0], vpads[1]))
            for k, (bi, h) in enumerate(pair):
                sl = slice(k * hd, (k + 1) * hd)
                o = r[GDN_CHUNK:, sl] + r2[GDN_TILE:, sl]
                states[c + k] = states[c + k] * eg_ref[bi, h, 8 * n:8 * n + 1, :] + r2[0:GDN_TILE, sl]
                on = o * lax.rsqrt(jnp.mean(o * o, axis=-1, keepdims=True) + EPS) * gn
                gate = _silu(zg_ref[bi, r0:r0 + GDN_CHUNK, h * hd:(h + 1) * hd])
                o_ref[bi, r0:r0 + GDN_CHUNK, h * hd:(h + 1) * hd] = (on * gate).astype(BF16)
    for c, (bi, h) in enumerate(chains):
        st_ref[bi, h] = states[c]


def _gdn_scan(u, wq, kdt, attn, eg, zg, gn, layer, *, ts):
    b, nh, seq, hd = u.shape
    hspec = lambda rows: pl.BlockSpec((b, nh, rows, hd), lambda ti: (0, 0, ti, 0))
    return pl.pallas_call(
        functools.partial(_gdn_scan_body, ts=ts),
        grid=(seq // ts,),
        in_specs=[hspec(ts), hspec(2 * ts), hspec(ts), hspec(ts), hspec(ts // 8),
                  pl.BlockSpec((b, ts, nh * hd), lambda ti: (0, ti, 0)),
                  _layer_spec((1, hd), layer, lambda ti: (0, 0))],
        out_specs=pl.BlockSpec((b, ts, nh * hd), lambda ti: (0, ti, 0)),
        out_shape=jax.ShapeDtypeStruct((b, seq, nh * hd), BF16),
        scratch_shapes=[pltpu.VMEM((b, nh, hd, hd), F32)],
        compiler_params=_cparams(("arbitrary",)),
        name="gdn_scan",
    )(u, wq, kdt, attn, eg, zg, gn)


def _ple_body(x_ref, p_ref, nw_ref, wg_ref, wp_ref, fn_ref, o_ref, *, final):
    x = x_ref[...]
    h = _rms(x, nw_ref[...]).astype(BF16)
    gate = _sigmoid(_dot(h, wg_ref[...]))
    out = x + gate * _dot(p_ref[...].astype(BF16), wp_ref[...])
    if final:
        out = _rms(out, fn_ref[...])
    o_ref[...] = out


def _ple(x, p, nw, wg, wp, fn, layer, *, tm, final):
    t, d = x.shape
    row = lambda i: (i, 0)
    return pl.pallas_call(
        functools.partial(_ple_body, final=final),
        grid=(t // tm,),
        in_specs=[pl.BlockSpec((tm, d), row), _layer_spec((tm, p.shape[2]), layer, row),
                  _layer_spec((1, d), layer, lambda i: (0, 0)),
                  _layer_spec(wg.shape[1:], layer, lambda i: (0, 0)),
                  _layer_spec(wp.shape[1:], layer, lambda i: (0, 0)),
                  pl.BlockSpec(fn.shape, lambda i: (0, 0))],
        out_specs=pl.BlockSpec((tm, d), row),
        out_shape=jax.ShapeDtypeStruct((t, d), F32),
        compiler_params=_cparams(("parallel",)),
        name="ple",
    )(x, p, nw, wg, wp, fn)


def _pack_w_in(w_in):
    offs = [0]
    for s in IN_SIZES:
        offs.append(offs[-1] + s)
    main = w_in[..., offs[0]:offs[7]]
    gates = w_in[..., offs[7]:offs[8]]
    gdn = w_in[..., offs[8]:offs[10]]
    ab = w_in[..., offs[10]:offs[12]]
    per_head = 3 * NSA_GROUP
    zpad = lambda n: jnp.zeros(w_in.shape[:-1] + (n,), w_in.dtype)
    gate_cols = []
    for hk in range(NSA_KV_HEADS):
        gate_cols += [gates[..., hk * per_head:(hk + 1) * per_head], zpad(GATE_ROWS - per_head)]
    tail = zpad(LANES - SMALL_GATE - ab.shape[-1])
    return jnp.concatenate([main, gdn] + gate_cols + [ab, tail], axis=-1).astype(BF16)


def _pack_cmp_w1(w1):
    depth = w1.shape[0]
    w1r = w1.reshape(depth, CMP_BLOCK, NSA_HEAD_DIM, CMP_HIDDEN).astype(BF16)
    z = jnp.zeros_like(w1r)
    return jnp.stack([jnp.concatenate([w1r, z], axis=2), jnp.concatenate([z, w1r], axis=2)], axis=1)


def _rope_tables(seq):
    dim = NSA_HEAD_DIM
    inv = 1.0 / (ROPE_THETA ** (jnp.arange(0, dim, 2, dtype=F32) / dim))
    ang = jnp.arange(seq, dtype=F32)[:, None] * inv[None, :]
    ang = jnp.concatenate([ang, ang], axis=-1)
    cos, sin = jnp.cos(ang), jnp.sin(ang)
    sign = jnp.where(jnp.arange(dim) < dim // 2, -1.0, 1.0).astype(F32)
    return jnp.tile(cos, (1, LANES // dim)), jnp.tile(sin * sign[None, :], (1, LANES // dim))


def _overlap_t(seq, nc_pad):
    n_slc = seq // SLC_BLOCK
    jc = jnp.arange(nc_pad)[None, :]
    js = jnp.arange(LANES)[:, None]
    ov = ((jc * CMP_STRIDE < (js + 1) * SLC_BLOCK) & (jc * CMP_STRIDE + CMP_BLOCK > js * SLC_BLOCK)
          & (js < n_slc))
    return ov.astype(BF16)


def _mask_bands(seq):
    q = (jnp.arange(NSA_GROUP * Q_BLOCK) % Q_BLOCK)[None, :]
    nc = seq // CMP_STRIDE
    rel = jnp.arange(2 * nc)[:, None] - nc
    cband = jnp.where(rel * CMP_STRIDE + (CMP_BLOCK - 1) <= q, 0.0, NEG_INF).astype(F32)
    r = jnp.arange(2 * WINDOW + Q_BLOCK)[:, None]
    wband = jnp.where((q < r) & (r <= q + WINDOW), 0.0, NEG_INF).astype(F32)
    return cband, wband


def _layer(x2, layer, w, consts, *, b, seq, final, cfg):
    t = b * seq
    x2 = _ffn(x2, w['ffn1_norm'], w['ffn1_w1'], w['ffn1_w3'], w['ffn1_w2'], layer,
              tm=cfg['ffn_tm'], tf=cfg['ffn_tf'])
    (q_t, kaug, vs_t, kw, vw_t, gates_t, cmpk, cmpv, qkv, zg, small) = _inproj(
        x2, w['mix_norm'], w['w_in'], consts['cos'], consts['sin'], layer, tm=cfg['in_tm'], b=b, seq=seq)
    kcmp, vcmp_t = _compress(cmpk.reshape(b, seq, LANES), cmpv.reshape(b, seq, LANES), w['cmp_pe'],
                             w['cmp_w1'], w['cmp_w2'], layer)
    o_nsa = _nsa(q_t, kcmp, vcmp_t, kaug, vs_t, kw, vw_t, gates_t, consts['ovt'], consts['cband'],
                 consts['wband'])
    u, wq, kdt, attn, eg = _gdn_prep(qkv.reshape(b, seq, 3 * GDN_WIDTH), w['gdn_conv'],
                                     small.reshape(b, seq, LANES), w['gdn_a_log'], w['gdn_dt_bias'],
                                     layer, ts=cfg['prep_ts'])
    o_gdn = _gdn_scan(u, wq, kdt, attn, eg, zg.reshape(b, seq, GDN_WIDTH), w['gdn_norm'], layer,
                      ts=cfg['scan_ts'])
    x2 = _ffn(x2, w['ffn2_norm'], w['ffn2_w1'], w['ffn2_w3'], w['ffn2_w2'], layer,
              tm=cfg['mix_tm'], tf=cfg['ffn_tf'],
              mix=(o_nsa.reshape(t, NSA_WIDTH), o_gdn.reshape(t, GDN_WIDTH), w['w_out']))
    return _ple(x2, w['p'], w['ple_norm'], w['ple_gate'], w['ple_proj'], consts['final_norm'], layer,
                tm=cfg['ple_tm'], final=final)


DEFAULT_CFG = dict(ffn_tm=1024, ffn_tf=1408, in_tm=512, prep_ts=256, scan_ts=256,
                   mix_tm=512, ple_tm=512)


def _forward(x, p, w, cfg):
    b, seq, d = x.shape
    depth = p.shape[0]
    t = b * seq
    cos2, sin2 = _rope_tables(seq)
    cband, wband = _mask_bands(seq)
    consts = dict(cos=cos2, sin=sin2, ovt=_overlap_t(seq, seq // CMP_STRIDE), cband=cband, wband=wband,
                  final_norm=w['final_norm'].reshape(1, d))
    bf = lambda a: a.astype(BF16)
    row3 = lambda a: a.reshape(depth, 1, a.shape[-1])
    lane_rows = lambda v, off: jnp.zeros((depth, 1, LANES), F32).at[:, 0, off:off + v.shape[1]].set(v)
    pe = jnp.stack([w['cmp_pe_k'], w['cmp_pe_v']], axis=1)
    w2 = jnp.stack([w['cmp_k_w2'], w['cmp_v_w2']], axis=1)
    ws = dict(
        p=p.reshape(depth, t, p.shape[-1]),
        ffn1_norm=row3(w['ffn1_norm']), ffn1_w1=bf(w['ffn1_w1']), ffn1_w3=bf(w['ffn1_w3']),
        ffn1_w2=bf(w['ffn1_w2']),
        mix_norm=row3(w['mix_norm']), w_in=_pack_w_in(w['w_in']),
        cmp_pe=jnp.concatenate([pe, pe], axis=-1),
        cmp_w1=jnp.stack([_pack_cmp_w1(w['cmp_k_w1']), _pack_cmp_w1(w['cmp_v_w1'])], axis=1),
        cmp_w2=bf(jnp.pad(w2, ((0, 0), (0, 0), (0, 0), (0, LANES - w2.shape[-1])))),
        gdn_conv=w['gdn_conv'], gdn_a_log=lane_rows(w['gdn_a_log'], SMALL_A),
        gdn_dt_bias=lane_rows(w['gdn_dt_bias'], SMALL_A), gdn_norm=row3(w['gdn_norm']),
        w_out=bf(w['w_out']),
        ffn2_norm=row3(w['ffn2_norm']), ffn2_w1=bf(w['ffn2_w1']), ffn2_w3=bf(w['ffn2_w3']),
        ffn2_w2=bf(w['ffn2_w2']),
        ple_norm=row3(w['ple_norm']), ple_gate=bf(w['ple_gate']), ple_proj=bf(w['ple_proj']),
    )
    x2 = x.reshape(t, d)
    for i in range(depth):
        x2 = _layer(x2, i, ws, consts, b=b, seq=seq, final=(i == depth - 1), cfg=cfg)
    return x2.reshape(b, seq, d)


def kernel(x, p, ffn1_norm, ffn1_w1, ffn1_w3, ffn1_w2, mix_norm, w_in, cmp_pe_k, cmp_pe_v,
           cmp_k_w1, cmp_k_w2, cmp_v_w1, cmp_v_w2, gdn_conv, gdn_a_log, gdn_dt_bias, gdn_norm,
           w_out, ffn2_norm, ffn2_w1, ffn2_w3, ffn2_w2, ple_norm, ple_gate, ple_proj, final_norm):
    w = dict(ffn1_norm=ffn1_norm, ffn1_w1=ffn1_w1, ffn1_w3=ffn1_w3, ffn1_w2=ffn1_w2,
             mix_norm=mix_norm, w_in=w_in, cmp_pe_k=cmp_pe_k, cmp_pe_v=cmp_pe_v,
             cmp_k_w1=cmp_k_w1, cmp_k_w2=cmp_k_w2, cmp_v_w1=cmp_v_w1, cmp_v_w2=cmp_v_w2,
             gdn_conv=gdn_conv, gdn_a_log=gdn_a_log, gdn_dt_bias=gdn_dt_bias, gdn_norm=gdn_norm,
             w_out=w_out, ffn2_norm=ffn2_norm, ffn2_w1=ffn2_w1, ffn2_w3=ffn2_w3, ffn2_w2=ffn2_w2,
             ple_norm=ple_norm, ple_gate=ple_gate, ple_proj=ple_proj, final_norm=final_norm)
    return _forward(x, p, w, DEFAULT_CFG)
```

```python
import functools

import jax
import jax.numpy as jnp
from jax import lax
from jax.experimental import pallas as pl
from jax.experimental.pallas import tpu as pltpu

F32 = jnp.float32
BF16 = jnp.bfloat16

D_MODEL = 1024
NSA_HEADS = 8
NSA_KV_HEADS = 2
NSA_GROUP = NSA_HEADS // NSA_KV_HEADS
NSA_HEAD_DIM = 64
CMP_BLOCK = 32
CMP_STRIDE = 16
CMP_HIDDEN = 128
SLC_BLOCK = 64
N_SELECTED = 16
WINDOW = 512
Q_BLOCK = 128
GDN_HEADS = 4
GDN_HEAD_DIM = 128
GDN_CHUNK = 64
CONV_WIDTH = 4
D_FF = 2816
PLE_DIM = 256
ROPE_THETA = 10000.0
EPS = 1e-6
FORCE_SCORE = 1e6
NEG_INF = -1e30

NSA_WIDTH = NSA_HEADS * NSA_HEAD_DIM
NSA_KV_WIDTH = NSA_KV_HEADS * NSA_HEAD_DIM
GDN_WIDTH = GDN_HEADS * GDN_HEAD_DIM
IN_SIZES = (NSA_WIDTH, NSA_KV_WIDTH, NSA_KV_WIDTH, NSA_KV_WIDTH, NSA_KV_WIDTH,
            NSA_KV_WIDTH, NSA_KV_WIDTH, 3 * NSA_HEADS, 3 * GDN_WIDTH, GDN_WIDTH,
            GDN_HEADS, GDN_HEADS)

LANES = 128
NSA_MAIN = NSA_WIDTH + 6 * NSA_KV_WIDTH
GDN_MAIN = 4 * GDN_WIDTH
GATE_ROWS = 16
SMALL_GATE = NSA_KV_HEADS * GATE_ROWS
SMALL_A = SMALL_GATE
SMALL_B = SMALL_GATE + GDN_HEADS
W_IN_PACKED = NSA_MAIN + GDN_MAIN + LANES
V_ROWS = NSA_HEAD_DIM + 16
SEL_BIAS = -2.0 ** 100
M_INIT = -3.0e38
LOG2_E = 1.4426950408889634
VMEM_LIMIT = 56 * 1024 * 1024


def _cparams(sem):
    return pltpu.CompilerParams(dimension_semantics=sem, vmem_limit_bytes=VMEM_LIMIT)


def _rms(x, w):
    ms = jnp.mean(x * x, axis=-1, keepdims=True)
    return x * lax.rsqrt(ms + EPS) * w


def _sigmoid(x):
    return 1.0 / (1.0 + jnp.exp(-x))


def _silu(x):
    return x * _sigmoid(x)


def _dot(a, b):
    return jnp.dot(a, b, preferred_element_type=F32)


def _dot_nt(a, b):
    return lax.dot_general(a, b, (((1,), (1,)), ((), ())), preferred_element_type=F32)


def _split3(x):
    hi = x.astype(BF16)
    r = x - hi.astype(F32)
    mid = r.astype(BF16)
    lo = (r - mid.astype(F32)).astype(BF16)
    return hi, mid, lo


def _dot_exact_lhs(a_bf, x):
    hi, mid, lo = _split3(x)
    return _dot(a_bf, hi) + (_dot(a_bf, mid) + _dot(a_bf, lo))


def _split2(x):
    hi = x.astype(BF16)
    return hi, (x - hi.astype(F32)).astype(BF16)


def _dot_split(a, b):
    ah, am = a
    bh, bm = b
    return _dot(ah, bh) + (_dot(ah, bm) + _dot(am, bh))


def _layer_spec(block, layer, index):
    return pl.BlockSpec((None,) + block, lambda *g: (layer,) + index(*g))


def _ffn_body(x_ref, nw_ref, w1_ref, w3_ref, w2_ref, *rest, mixed, final):
    if mixed:
        a_ref, b_ref, wo_ref, p_ref, pn_ref, wg_ref, wp_ref, fn_ref, o_ref, h_ref, xs_ref = rest
    else:
        o_ref, h_ref = rest
        xs_ref = x_ref
    j = pl.program_id(1)

    @pl.when(j == 0)
    def _():
        if mixed:
            half = a_ref.shape[1]
            xs_ref[...] = (x_ref[...] + _dot(a_ref[...], wo_ref[0:half, :])
                           + _dot(b_ref[...], wo_ref[half:, :]))
        h_ref[...] = _rms(xs_ref[...], nw_ref[...]).astype(BF16)
        o_ref[...] = jnp.zeros_like(o_ref)

    h = h_ref[...]
    u = _dot(h, w1_ref[...])
    g = _dot(h, w3_ref[...])
    a = (_silu(u) * g).astype(BF16)
    o_ref[...] += _dot(a, w2_ref[...])

    @pl.when(j == pl.num_programs(1) - 1)
    def _():
        out = xs_ref[...] + 0.5 * o_ref[...]
        if mixed:
            hp = _rms(out, pn_ref[...]).astype(BF16)
            gate = _sigmoid(_dot(hp, wg_ref[...]))
            out = out + gate * _dot(p_ref[...].astype(BF16), wp_ref[...])
            if final:
                out = _rms(out, fn_ref[...])
        o_ref[...] = out


def _ffn(x, nw, w1, w3, w2, layer, *, tm, tf, mix=None, final=False):
    t, d = x.shape
    ff = w1.shape[2]
    row = lambda i, j: (i, 0)
    in_specs = [
        pl.BlockSpec((tm, d), row),
        _layer_spec((1, d), layer, lambda i, j: (0, 0)),
        _layer_spec((d, tf), layer, lambda i, j: (0, j)),
        _layer_spec((d, tf), layer, lambda i, j: (0, j)),
        _layer_spec((tf, d), layer, lambda i, j: (j, 0)),
    ]
    scratch = [pltpu.VMEM((tm, d), BF16)]
    args = (x, nw, w1, w3, w2)
    if mix is not None:
        a, b, w_out, p, pn, wg, wp, fn = mix
        in_specs += [pl.BlockSpec((tm, a.shape[1]), row), pl.BlockSpec((tm, b.shape[1]), row),
                     _layer_spec(w_out.shape[1:], layer, lambda i, j: (0, 0)),
                     _layer_spec((tm, p.shape[2]), layer, row),
                     _layer_spec((1, d), layer, lambda i, j: (0, 0)),
                     _layer_spec(wg.shape[1:], layer, lambda i, j: (0, 0)),
                     _layer_spec(wp.shape[1:], layer, lambda i, j: (0, 0)),
                     pl.BlockSpec(fn.shape, lambda i, j: (0, 0))]
        scratch.append(pltpu.VMEM((tm, d), F32))
        args += mix
    return pl.pallas_call(
        functools.partial(_ffn_body, mixed=mix is not None, final=final),
        grid=(t // tm, ff // tf),
        in_specs=in_specs,
        out_specs=pl.BlockSpec((tm, d), row),
        out_shape=jax.ShapeDtypeStruct((t, d), F32),
        scratch_shapes=scratch,
        compiler_params=_cparams(("parallel", "arbitrary")),
        name="ffn_mix" if mix is not None else "ffn",
    )(*args)


def _rope(xg, cos, sin_signed, first_half):
    fwd = pltpu.roll(xg, LANES - NSA_HEAD_DIM // 2, 1)
    bwd = pltpu.roll(xg, NSA_HEAD_DIM // 2, 1)
    return xg * cos + jnp.where(first_half, fwd, bwd) * sin_signed


def _inproj_body(x_ref, nw_ref, w_ref, cos_ref, sin_ref,
                 qt_ref, kaug_ref, vst_ref, kw_ref, vwt_ref, gt_ref,
                 cmpk_ref, cmpv_ref, qkv_ref, zg_ref, small_ref, *, nseq):
    tm = x_ref.shape[0]
    nq = tm // Q_BLOCK
    dh = NSA_HEAD_DIM
    h = _rms(x_ref[...], nw_ref[...]).astype(BF16)
    cos = cos_ref[...]
    sin_s = sin_ref[...]
    lane = lax.broadcasted_iota(jnp.int32, (1, LANES), 1)
    first_half = (lane & (dh - 1)) < (dh // 2)
    low = lane < dh

    z = _dot(h, w_ref[:, 0:NSA_MAIN])
    scale = dh ** -0.5 * LOG2_E
    for pair in range(NSA_WIDTH // LANES):
        zq = _rope(z[:, pair * LANES:(pair + 1) * LANES], cos, sin_s, first_half) * scale
        tr = jnp.transpose(zq).astype(BF16)
        for half in range(2):
            hk, g = divmod(2 * pair + half, NSA_GROUP)
            for qb in range(nq):
                qt_ref[0, hk, qb, 0:dh, g * Q_BLOCK:(g + 1) * Q_BLOCK] = (
                    tr[half * dh:(half + 1) * dh, qb * Q_BLOCK:(qb + 1) * Q_BLOCK])
    qt_ref[0, :, :, dh:, :] = jnp.zeros((NSA_KV_HEADS, nq, LANES - dh, NSA_GROUP * Q_BLOCK), BF16)

    def group(c, rotary):
        zc = z[:, NSA_WIDTH + c * LANES: NSA_WIDTH + (c + 1) * LANES]
        return _rope(zc, cos, sin_s, first_half) if rotary else zc

    def heads(zc):
        return [jnp.where(low, zc, 0.0), jnp.where(low, pltpu.roll(zc, dh, 1), 0.0)]

    cmpk_ref[...] = group(0, True)
    cmpv_ref[...] = group(1, False)
    tok = (pl.program_id(0) % nseq) * tm + lax.broadcasted_iota(jnp.int32, (tm, 1), 0)
    local_blk = jnp.right_shift(tok, SLC_BLOCK.bit_length() - 1) & (tm // SLC_BLOCK - 1)
    onehot = jnp.where(lane - dh == local_blk, 1.0, 0.0)
    for hk, kh in enumerate(heads(group(2, True))):
        kaug_ref[0, hk] = jnp.where(low, kh, onehot).astype(BF16)
    for hk, kh in enumerate(heads(group(4, True))):
        kw_ref[0, hk] = kh.astype(BF16)
    vst = jnp.transpose(group(3, False)).astype(BF16)
    ones_rows = jnp.where(lax.broadcasted_iota(jnp.int32, (V_ROWS - dh, tm), 0) == 0, 1.0, 0.0)
    vwt = jnp.transpose(group(5, False)).astype(BF16)
    for hk in range(NSA_KV_HEADS):
        vst_ref[0, hk, 0] = jnp.concatenate([vst[hk * dh:(hk + 1) * dh], ones_rows.astype(BF16)], axis=0)
        for qb in range(nq):
            vwt_ref[0, hk, qb] = vwt[hk * dh:(hk + 1) * dh, qb * Q_BLOCK:(qb + 1) * Q_BLOCK]

    zg = _dot(h, w_ref[:, NSA_MAIN:NSA_MAIN + GDN_MAIN])
    qkv_ref[...] = zg[:, 0:3 * GDN_WIDTH]
    zg_ref[...] = zg[:, 3 * GDN_WIDTH:]

    zs = _dot(h, w_ref[:, NSA_MAIN + GDN_MAIN:])
    is_raw = (lane >= SMALL_A) & (lane < SMALL_B)
    small = jnp.where(is_raw, zs, _sigmoid(zs))
    small_ref[...] = small
    small_t = jnp.transpose(small)
    for qb in range(nq):
        gt_ref[0, qb] = small_t[0:SMALL_GATE, qb * Q_BLOCK:(qb + 1) * Q_BLOCK]


def _inproj(x, nw, w_packed, cos2, sin2, layer, *, tm, b, seq):
    t, d = x.shape
    nseq = seq // tm
    nq = tm // Q_BLOCK
    nqb = seq // Q_BLOCK
    hkv, dh = NSA_KV_HEADS, NSA_HEAD_DIM
    cols = NSA_GROUP * Q_BLOCK
    row = lambda i: (i, 0)
    tile5 = lambda i: (i // nseq, 0, i % nseq, 0, 0)
    tile4 = lambda i: (i // nseq, 0, i % nseq, 0)
    return pl.pallas_call(
        functools.partial(_inproj_body, nseq=nseq),
        grid=(t // tm,),
        in_specs=[
            pl.BlockSpec((tm, d), row),
            _layer_spec((1, d), layer, lambda i: (0, 0)),
            _layer_spec((d, W_IN_PACKED), layer, lambda i: (0, 0)),
            pl.BlockSpec((tm, LANES), lambda i: (i % nseq, 0)),
            pl.BlockSpec((tm, LANES), lambda i: (i % nseq, 0)),
        ],
        out_specs=[
            pl.BlockSpec((1, hkv, nq, LANES, cols), tile5),
            pl.BlockSpec((1, hkv, tm, LANES), tile4),
            pl.BlockSpec((1, hkv, 1, V_ROWS, tm), tile5),
            pl.BlockSpec((1, hkv, tm, LANES), tile4),
            pl.BlockSpec((1, hkv, nq, dh, Q_BLOCK), tile5),
            pl.BlockSpec((1, nq, SMALL_GATE, Q_BLOCK), lambda i: (i // nseq, i % nseq, 0, 0)),
            pl.BlockSpec((tm, NSA_KV_WIDTH), row),
            pl.BlockSpec((tm, NSA_KV_WIDTH), row),
            pl.BlockSpec((tm, 3 * GDN_WIDTH), row),
            pl.BlockSpec((tm, GDN_WIDTH), row),
            pl.BlockSpec((tm, LANES), row),
        ],
        out_shape=[
            jax.ShapeDtypeStruct((b, hkv, nqb, LANES, cols), BF16),
            jax.ShapeDtypeStruct((b, hkv, seq, LANES), BF16),
            jax.ShapeDtypeStruct((b, hkv, nseq, V_ROWS, tm), BF16),
            jax.ShapeDtypeStruct((b, hkv, seq, LANES), BF16),
            jax.ShapeDtypeStruct((b, hkv, nqb, dh, Q_BLOCK), BF16),
            jax.ShapeDtypeStruct((b, nqb, SMALL_GATE, Q_BLOCK), F32),
            jax.ShapeDtypeStruct((t, NSA_KV_WIDTH), F32),
            jax.ShapeDtypeStruct((t, NSA_KV_WIDTH), F32),
            jax.ShapeDtypeStruct((t, 3 * GDN_WIDTH), F32),
            jax.ShapeDtypeStruct((t, GDN_WIDTH), F32),
            jax.ShapeDtypeStruct((t, LANES), F32),
        ],
        compiler_params=_cparams(("parallel",)),
        name="inproj",
    )(x, nw, w_packed, cos2, sin2)


def _compress_body(xk_ref, xv_ref, pe_ref, w1_ref, w2_ref, ok_ref, ov_ref):
    nh = ok_ref.shape[2]
    hkv = NSA_KV_HEADS
    lo = [[jnp.zeros((nh, CMP_HIDDEN), F32) for _ in range(hkv)] for _ in range(2)]
    hi = [[jnp.zeros((nh, CMP_HIDDEN), F32) for _ in range(hkv)] for _ in range(2)]
    for l in range(CMP_STRIDE):
        for kv, x_ref in enumerate((xk_ref, xv_ref)):
            xg = x_ref[0, pl.ds(l, nh, stride=CMP_STRIDE), :]
            x_lo = (xg + pe_ref[kv, l:l + 1, :]).astype(BF16)
            x_hi = (xg + pe_ref[kv, CMP_STRIDE + l:CMP_STRIDE + l + 1, :]).astype(BF16)
            for hk in range(hkv):
                lo[kv][hk] = lo[kv][hk] + _dot(x_lo, w1_ref[kv, hk, l])
                hi[kv][hk] = hi[kv][hk] + _dot(x_hi, w1_ref[kv, hk, CMP_STRIDE + l])
    for hk in range(hkv):
        hid_k = lo[0][hk] + pltpu.roll(hi[0][hk], nh - 1, 0)
        hid_v = lo[1][hk] + pltpu.roll(hi[1][hk], nh - 1, 0)
        ok_ref[0, hk] = _dot(_silu(hid_k).astype(BF16), w2_ref[0]).astype(BF16)
        vt = jnp.transpose(_dot(_silu(hid_v).astype(BF16), w2_ref[1]))
        ov_ref[0, hk] = vt[0:NSA_HEAD_DIM].astype(BF16)


def _compress(xk, xv, pe, w1, w2, layer):
    b, seq, wide = xk.shape
    nh = seq // CMP_STRIDE
    hkv = NSA_KV_HEADS
    return pl.pallas_call(
        _compress_body,
        grid=(b,),
        in_specs=[pl.BlockSpec((1, seq, wide), lambda i: (i, 0, 0)),
                  pl.BlockSpec((1, seq, wide), lambda i: (i, 0, 0)),
                  _layer_spec(pe.shape[1:], layer, lambda i: (0, 0, 0)),
                  _layer_spec(w1.shape[1:], layer, lambda i: (0, 0, 0, 0, 0)),
                  _layer_spec(w2.shape[1:], layer, lambda i: (0, 0, 0))],
        out_specs=[pl.BlockSpec((1, hkv, nh, LANES), lambda i: (i, 0, 0, 0)),
                   pl.BlockSpec((1, hkv, NSA_HEAD_DIM, nh), lambda i: (i, 0, 0, 0))],
        out_shape=[jax.ShapeDtypeStruct((b, hkv, nh, LANES), BF16),
                   jax.ShapeDtypeStruct((b, hkv, NSA_HEAD_DIM, nh), BF16)],
        compiler_params=_cparams(("parallel",)),
        name="compress",
    )(xk, xv, pe, w1, w2)


def _colmax(s):
    r = s.shape[0]
    while r > 8 and r % 4 == 0:
        r //= 4
        s = jnp.max(s.reshape(4, r, s.shape[-1]), axis=0)
    return jnp.max(s, axis=0, keepdims=True)


def _nsa_body(q_ref, kc_ref, vc_ref, kaug_ref, vs_ref, kw_ref, vw_ref, g_ref, ovt_ref, cband_ref,
              wband_ref, o_ref, s_ref, b_ref, *, n_slc, n_sel, tk):
    i = pl.program_id(1)
    t0 = i * Q_BLOCK
    cols = NSA_GROUP * Q_BLOCK
    dh = NSA_HEAD_DIM
    heads = range(NSA_KV_HEADS)
    q_t = [q_ref[0, h, 0] for h in heads]
    lane = lax.broadcasted_iota(jnp.int32, (1, cols), 1)
    tq = t0 + (lane & (Q_BLOCK - 1))

    def softmax_cols(s, zero=None):
        m = _colmax(s)
        if zero is not None:
            m = m + zero
        e = jnp.exp2(s - m)
        return e, jnp.sum(e, axis=0, keepdims=True)

    nc = kc_ref.shape[2]
    per_q = Q_BLOCK // CMP_STRIDE
    cband = cband_ref[pl.ds(pl.multiple_of(nc - per_q * i, per_q), nc), :]
    wspan = Q_BLOCK + WINDOW
    c0 = jnp.maximum(i - WINDOW // Q_BLOCK, 0)
    start = pl.multiple_of(c0 * Q_BLOCK, Q_BLOCK)
    shift = pl.multiple_of(jnp.maximum(WINDOW - t0, 0), Q_BLOCK)
    wband = wband_ref[pl.ds(shift, wspan), :]
    o_cmp, o_win, imps = [], [], []
    for h in heads:
        e_c, l_c = softmax_cols(_dot(kc_ref[0, h], q_t[h]) + cband)
        inv_c = jnp.where(tq >= CMP_BLOCK - 1, 1.0 / l_c, 0.0)
        o_cmp.append(_dot(vc_ref[0, h], e_c.astype(BF16)) * inv_c)
        p_c = e_c * inv_c
        pg = p_c[:, 0:Q_BLOCK]
        for g in range(1, NSA_GROUP):
            pg = pg + p_c[:, g * Q_BLOCK:(g + 1) * Q_BLOCK]
        imp = _dot_exact_lhs(ovt_ref[...], pg)
        imps.append(imp)
        zero = jnp.concatenate([jnp.where(imp[0:1, :] > 1e30, 1.0, 0.0)] * NSA_GROUP, axis=1)
        e_w, l_w = softmax_cols(_dot(kw_ref[0, h, pl.ds(start, wspan), :], q_t[h]) + wband, zero)
        e_w = e_w.astype(BF16)
        ow = _dot(vw_ref[0, h, c0], e_w[0:Q_BLOCK])
        for c in range(1, wspan // Q_BLOCK):
            ow = ow + _dot(vw_ref[0, h, c0 + c], e_w[c * Q_BLOCK:(c + 1) * Q_BLOCK])
        o_win.append(ow * (1.0 / l_w))

    blk = lax.broadcasted_iota(jnp.int32, (LANES, Q_BLOCK), 0)
    tcol = t0 + lax.broadcasted_iota(jnp.int32, (LANES, Q_BLOCK), 1)
    cur = jnp.right_shift(tcol, SLC_BLOCK.bit_length() - 1)
    forced = (blk == 0) | (blk == cur) | (blk == cur - 1)
    valid = blk * SLC_BLOCK <= tcol
    score = [jnp.where((blk < n_slc) & jnp.logical_not(forced), jnp.where(valid, imps[h], -1.0), -jnp.inf)
             for h in heads]
    bias = [jnp.where(forced, 0.0, SEL_BIAS) for _ in heads]
    for _ in range(max(n_sel - 3, 0)):
        for h in heads:
            mx = jnp.max(score[h], axis=0, keepdims=True)
            idx = jnp.min(jnp.where(score[h] == mx, blk, 2 * LANES), axis=0, keepdims=True)
            hit = blk == idx
            bias[h] = jnp.where(hit, 0.0, bias[h])
            score[h] = jnp.where(hit, -jnp.inf, score[h])
    for h in heads:
        b_ref[h] = jnp.concatenate([bias[h]] * NSA_GROUP, axis=1)

    nb = tk // SLC_BLOCK
    pad = jnp.zeros((16 - nb, cols), F32)

    def scores(h, j):
        rows = jnp.concatenate([b_ref[h, pl.ds(pl.multiple_of(j * nb, nb), nb), :], pad], axis=0)
        q_aug = jnp.concatenate([q_t[h][0:dh], rows.astype(BF16), q_t[h][dh + 16:]], axis=0)
        return _dot(kaug_ref[0, h, pl.ds(pl.multiple_of(j * tk, tk), tk), :], q_aug)

    def absorb(h, j, slot, m, acc, causal):
        s = s_ref[h, slot]
        if causal:
            kpos = j * tk + lax.broadcasted_iota(jnp.int32, (tk, 1), 0)
            s = jnp.where(kpos <= tq, s, NEG_INF)
        m_new = jnp.maximum(m, _colmax(s))
        p = jnp.exp2(s - m_new).astype(BF16)
        return m_new, jnp.exp2(m - m_new) * acc + _dot(vs_ref[0, h, j], p)

    def absorb_all(j, slot, carry, causal):
        out = ()
        for h in heads:
            out += absorb(h, j, slot, carry[2 * h], carry[2 * h + 1], causal)
        return out

    def fill(slot, j):
        for h in heads:
            s_ref[h, slot] = scores(h, j)

    def slc_pair(jj, carry):
        j = 2 * jj
        fill(1, j + 1)
        carry = absorb_all(j, 0, carry, False)
        fill(0, j + 2)
        return absorb_all(j + 1, 1, carry, False)

    def tail_odd(*carry):
        fill(1, n_full)
        carry = absorb_all(n_full - 1, 0, carry, False)
        return absorb_all(n_full, 1, carry, True)

    def tail_even(*carry):
        return absorb_all(n_full, 0, carry, True)

    n_full = t0 // tk
    fill(0, 0)
    init = (jnp.full((1, cols), M_INIT, F32), jnp.zeros((V_ROWS, cols), F32)) * NSA_KV_HEADS
    carry = lax.fori_loop(0, n_full // 2, slc_pair, init)
    carry = lax.cond(n_full % 2 == 1, tail_odd, tail_even, *carry)

    pairs = []
    for h in heads:
        acc_s = carry[2 * h + 1]
        o_slc = acc_s[0:dh] * (1.0 / acc_s[dh:dh + 1])
        gate = g_ref[0, 0, h * GATE_ROWS:(h + 1) * GATE_ROWS]
        outs = []
        for g in range(NSA_GROUP):
            sl = slice(g * Q_BLOCK, (g + 1) * Q_BLOCK)
            outs.append(gate[3 * g:3 * g + 1] * o_cmp[h][:, sl] + gate[3 * g + 1:3 * g + 2] * o_slc[:, sl]
                        + gate[3 * g + 2:3 * g + 3] * o_win[h][:, sl])
        pairs += [jnp.transpose(jnp.concatenate(outs[2 * k:2 * k + 2], axis=0))
                  for k in range(NSA_GROUP // 2)]
    o_ref[0] = jnp.concatenate(pairs, axis=1).astype(BF16)


def _nsa(q_t, kcmp, vcmp_t, kaug, vs_t, kw, vw_t, gates_t, ovt, cband, wband):
    b, hkv, nqb, _, cols = q_t.shape
    seq = kaug.shape[2]
    tk = vs_t.shape[4]
    n_slc = seq // SLC_BLOCK
    per_batch = lambda a: pl.BlockSpec((1,) + a.shape[1:], lambda bi, qi: (bi,) + (0,) * (a.ndim - 1))
    const = lambda a: pl.BlockSpec(a.shape, lambda bi, qi: (0, 0))
    body = functools.partial(_nsa_body, n_slc=n_slc, n_sel=min(N_SELECTED, n_slc), tk=tk)
    return pl.pallas_call(
        body,
        grid=(b, nqb),
        in_specs=[pl.BlockSpec((1, hkv, 1) + q_t.shape[3:], lambda bi, qi: (bi, 0, qi, 0, 0)),
                  per_batch(kcmp), per_batch(vcmp_t), per_batch(kaug), per_batch(vs_t),
                  per_batch(kw), per_batch(vw_t),
                  pl.BlockSpec((1, 1, SMALL_GATE, Q_BLOCK), lambda bi, qi: (bi, qi, 0, 0)),
                  const(ovt), const(cband), const(wband)],
        out_specs=pl.BlockSpec((1, Q_BLOCK, NSA_WIDTH), lambda bi, qi: (bi, qi, 0)),
        out_shape=jax.ShapeDtypeStruct((b, seq, NSA_WIDTH), BF16),
        scratch_shapes=[pltpu.VMEM((hkv, 2, tk, cols), F32), pltpu.VMEM((hkv, LANES, cols), F32)],
        compiler_params=_cparams(("parallel", "arbitrary")),
        name="nsa",
    )(q_t, kcmp, vcmp_t, kaug, vs_t, kw, vw_t, gates_t, ovt, cband, wband)


GDN_TILE = 2 * GDN_CHUNK


def _gdn_prep_body(qkv_ref, halo_ref, cw_ref, small_ref, alog_ref, dtb_ref,
                   u_ref, wq_ref, kdt_ref, attn_ref, eg_ref, *, ts):
    i = pl.program_id(1)
    x = qkv_ref[0]
    halo = jnp.where(i > 0, halo_ref[0], 0.0)
    xx = jnp.concatenate([halo, x], axis=0)
    y = x * cw_ref[CONV_WIDTH - 1:CONV_WIDTH, :]
    for d in range(1, CONV_WIDTH):
        shifted = pltpu.roll(xx, d, 0)[8:]
        y = y + shifted * cw_ref[CONV_WIDTH - 1 - d:CONV_WIDTH - d, :]
    y = _silu(y)

    sm = small_ref[0]
    sp_in = sm + dtb_ref[...]
    softplus = jnp.maximum(sp_in, 0.0) + jnp.log(1.0 + jnp.exp(-jnp.abs(sp_in)))
    glog = -jnp.exp(alog_ref[...]) * softplus

    ri = lax.broadcasted_iota(jnp.int32, (ts, ts), 0)
    ci = lax.broadcasted_iota(jnp.int32, (ts, ts), 1)
    sh = GDN_CHUNK.bit_length() - 1
    same = jnp.right_shift(ri, sh) == jnp.right_shift(ci, sh)
    tril = jnp.where(same & (ri >= ci), 1.0, 0.0).astype(BF16)
    ones = jnp.where(same, 1.0, 0.0).astype(BF16)
    gcum = _dot_exact_lhs(tril, glog)
    glast = _dot_exact_lhs(ones, glog)

    r2 = lax.broadcasted_iota(jnp.int32, (GDN_TILE, GDN_TILE), 0)
    c2 = lax.broadcasted_iota(jnp.int32, (GDN_TILE, GDN_TILE), 1)
    same2 = jnp.right_shift(r2, sh) == jnp.right_shift(c2, sh)
    incl = same2 & (r2 >= c2)
    strict = same2 & (r2 > c2)
    eye = jnp.where(r2 == c2, 1.0, 0.0)
    qscale = GDN_HEAD_DIM ** -0.5

    units = []
    for c in range(ts // GDN_TILE):
        r0 = c * GDN_TILE
        gc_tile = gcum[r0:r0 + GDN_TILE]
        gc_rows = jnp.transpose(gc_tile)
        for h in range(GDN_HEADS):
            lo = h * GDN_HEAD_DIM
            qh = y[r0:r0 + GDN_TILE, lo:lo + GDN_HEAD_DIM]
            kh = y[r0:r0 + GDN_TILE, GDN_WIDTH + lo:GDN_WIDTH + lo + GDN_HEAD_DIM]
            vh = y[r0:r0 + GDN_TILE, 2 * GDN_WIDTH + lo:2 * GDN_WIDTH + lo + GDN_HEAD_DIM]
            qh = qh * lax.rsqrt(jnp.sum(qh * qh, axis=-1, keepdims=True) + EPS)
            kh = kh * lax.rsqrt(jnp.sum(kh * kh, axis=-1, keepdims=True) + EPS)
            gc_col = gc_tile[:, SMALL_A + h:SMALL_A + h + 1]
            gc_row = gc_rows[SMALL_A + h:SMALL_A + h + 1, :]
            gl_col = glast[r0:r0 + GDN_TILE, SMALL_A + h:SMALL_A + h + 1]
            beta = sm[r0:r0 + GDN_TILE, SMALL_B + h:SMALL_B + h + 1]

            decay = jnp.where(incl, jnp.exp(jnp.minimum(gc_col - gc_row, 0.0)), 0.0)
            kb = kh * beta
            k_bf = kh.astype(BF16)
            a_s = jnp.where(strict, _dot_nt(kb.astype(BF16), k_bf) * decay, 0.0)
            egc = jnp.exp(gc_col)
            qs = qh * qscale
            attn = jnp.where(incl, _dot_nt(qs.astype(BF16), k_bf) * decay, 0.0)
            attn_ref[0, h, r0:r0 + GDN_TILE, :] = attn.astype(BF16)
            k_dec = kh * jnp.exp(gl_col - gc_col)
            kdt_ref[0, h, r0:r0 + GDN_TILE, :] = jnp.transpose(k_dec).astype(BF16)
            for cc in range(2):
                e0 = (r0 // GDN_CHUNK + cc) * 8
                eg_ref[0, h, e0:e0 + 8, :] = jnp.broadcast_to(
                    jnp.exp(gl_col[cc * GDN_CHUNK:cc * GDN_CHUNK + 8]), (8, GDN_HEAD_DIM))
            units.append(dict(h=h, r0=r0, a=a_s, rhs_u=vh * beta, rhs_w=kb * egc, q_dec=qs * egc))

    xinv = [eye - un['a'] for un in units]
    asp = [_split2(un['a']) for un in units]
    pw = [_dot_split(a, a) for a in asp]
    steps = GDN_CHUNK.bit_length() - 2
    for s in range(steps):
        pws = [_split2(p) for p in pw]
        xinv = [x + _dot_split(_split2(x), p) for x, p in zip(xinv, pws)]
        if s + 1 < steps:
            pw = [_dot_split(p, p) for p in pws]

    for un, x in zip(units, xinv):
        h, r0 = un['h'], un['r0']
        xs = _split2(x)
        u = _dot_split(xs, _split2(un['rhs_u']))
        w = _dot_split(xs, _split2(un['rhs_w']))
        u_ref[0, h, r0:r0 + GDN_TILE, :] = u
        for cc in range(2):
            a0 = cc * GDN_CHUNK
            wq = jnp.concatenate([w[a0:a0 + GDN_CHUNK], un['q_dec'][a0:a0 + GDN_CHUNK]], axis=0)
            n0 = 2 * r0 + cc * GDN_TILE
            wq_ref[0, h, n0:n0 + GDN_TILE, :] = wq.astype(BF16)


def _gdn_prep(qkv, cw, small, alog_row, dtb_row, layer, *, ts):
    b, seq, wide = qkv.shape
    nt = seq // ts
    hd = GDN_HEAD_DIM
    hspec = lambda rows: pl.BlockSpec((1, GDN_HEADS, rows, hd), lambda bi, ti: (bi, 0, ti, 0))
    hshape = lambda rows, dt: jax.ShapeDtypeStruct((b, GDN_HEADS, rows, hd), dt)
    return pl.pallas_call(
        functools.partial(_gdn_prep_body, ts=ts),
        grid=(b, nt),
        in_specs=[
            pl.BlockSpec((1, ts, wide), lambda bi, ti: (bi, ti, 0)),
            pl.BlockSpec((1, 8, wide), lambda bi, ti: (bi, jnp.maximum(ti * (ts // 8) - 1, 0), 0)),
            _layer_spec(cw.shape[1:], layer, lambda bi, ti: (0, 0)),
            pl.BlockSpec((1, ts, LANES), lambda bi, ti: (bi, ti, 0)),
            _layer_spec((1, LANES), layer, lambda bi, ti: (0, 0)),
            _layer_spec((1, LANES), layer, lambda bi, ti: (0, 0)),
        ],
        out_specs=[hspec(ts), hspec(2 * ts), hspec(ts), hspec(ts), hspec(ts // 8)],
        out_shape=[hshape(seq, F32), hshape(2 * seq, BF16), hshape(seq, BF16),
                   hshape(seq, BF16), hshape(seq // 8, F32)],
        compiler_params=_cparams(("parallel", "parallel")),
        name="gdn_prep",
    )(qkv, qkv, cw, small, alog_row, dtb_row)


def _gdn_scan_body(u_ref, wq_ref, kdt_ref, attn_ref, eg_ref, zg_ref, gn_ref, o_ref, st_ref, *, ts):
    @pl.when(pl.program_id(0) == 0)
    def _():
        st_ref[...] = jnp.zeros_like(st_ref)

    hd = GDN_HEAD_DIM
    gn = gn_ref[...]
    zeros = jnp.zeros((GDN_CHUNK, hd), F32)
    chains = [(bi, h) for bi in range(u_ref.shape[0]) for h in range(GDN_HEADS)]
    states = [st_ref[bi, h] for bi, h in chains]
    zsq = jnp.zeros((hd, hd), BF16)

    def block_diag(a, b):
        return jnp.concatenate([jnp.concatenate([a, zsq], axis=1),
                                jnp.concatenate([zsq, b], axis=1)], axis=0)

    for n in range(ts // GDN_CHUNK):
        r0 = n * GDN_CHUNK
        t0 = (n // 2) * GDN_TILE
        for c in range(0, len(chains), 2):
            pair = chains[c:c + 2]
            wq = jnp.concatenate([wq_ref[bi, h, 2 * r0:2 * r0 + GDN_TILE, :] for bi, h in pair], axis=1)
            r = _dot(wq, block_diag(states[c].astype(BF16), states[c + 1].astype(BF16)))
            vpads = []
            for k, (bi, h) in enumerate(pair):
                v_new = u_ref[bi, h, r0:r0 + GDN_CHUNK, :] - r[0:GDN_CHUNK, k * hd:(k + 1) * hd]
                vpad = jnp.concatenate([v_new, zeros] if n % 2 == 0 else [zeros, v_new], axis=0)
                vpads.append(vpad.astype(BF16))
            lhs = jnp.concatenate(
                [jnp.concatenate([kdt_ref[bi, h, t0:t0 + GDN_TILE, :] for bi, h in pair], axis=1),
                 jnp.concatenate([attn_ref[bi, h, r0:r0 + GDN_CHUNK, :] for bi, h in pair], axis=1)], axis=0)
            r2 = _dot(lhs, block_diag(vpads[0], vpads[1]))
            for k, (bi, h) in enumerate(pair):
                sl = slice(k * hd, (k + 1) * hd)
                o = r[GDN_CHUNK:, sl] + r2[GDN_TILE:, sl]
                states[c + k] = states[c + k] * eg_ref[bi, h, 8 * n:8 * n + 1, :] + r2[0:GDN_TILE, sl]
                on = o * lax.rsqrt(jnp.mean(o * o, axis=-1, keepdims=True) + EPS) * gn
                gate = _silu(zg_ref[bi, r0:r0 + GDN_CHUNK, h * hd:(h + 1) * hd])
                o_ref[bi, r0:r0 + GDN_CHUNK, h * hd:(h + 1) * hd] = (on * gate).astype(BF16)
    for c, (bi, h) in enumerate(chains):
        st_ref[bi, h] = states[c]


def _gdn_scan(u, wq, kdt, attn, eg, zg, gn, layer, *, ts):
    b, nh, seq, hd = u.shape
    hspec = lambda rows: pl.BlockSpec((b, nh, rows, hd), lambda ti: (0, 0, ti, 0))
    return pl.pallas_call(
        functools.partial(_gdn_scan_body, ts=ts),
        grid=(seq // ts,),
        in_specs=[hspec(ts), hspec(2 * ts), hspec(ts), hspec(ts), hspec(ts // 8),
                  pl.BlockSpec((b, ts, nh * hd), lambda ti: (0, ti, 0)),
                  _layer_spec((1, hd), layer, lambda ti: (0, 0))],
        out_specs=pl.BlockSpec((b, ts, nh * hd), lambda ti: (0, ti, 0)),
        out_shape=jax.ShapeDtypeStruct((b, seq, nh * hd), BF16),
        scratch_shapes=[pltpu.VMEM((b, nh, hd, hd), F32)],
        compiler_params=_cparams(("arbitrary",)),
        name="gdn_scan",
    )(u, wq, kdt, attn, eg, zg, gn)


def _ple_body(x_ref, p_ref, nw_ref, wg_ref, wp_ref, fn_ref, o_ref, *, final):
    x = x_ref[...]
    h = _rms(x, nw_ref[...]).astype(BF16)
    gate = _sigmoid(_dot(h, wg_ref[...]))
    out = x + gate * _dot(p_ref[...].astype(BF16), wp_ref[...])
    if final:
        out = _rms(out, fn_ref[...])
    o_ref[...] = out


def _ple(x, p, nw, wg, wp, fn, layer, *, tm, final):
    t, d = x.shape
    row = lambda i: (i, 0)
    return pl.pallas_call(
        functools.partial(_ple_body, final=final),
        grid=(t // tm,),
        in_specs=[pl.BlockSpec((tm, d), row), _layer_spec((tm, p.shape[2]), layer, row),
                  _layer_spec((1, d), layer, lambda i: (0, 0)),
                  _layer_spec(wg.shape[1:], layer, lambda i: (0, 0)),
                  _layer_spec(wp.shape[1:], layer, lambda i: (0, 0)),
                  pl.BlockSpec(fn.shape, lambda i: (0, 0))],
        out_specs=pl.BlockSpec((tm, d), row),
        out_shape=jax.ShapeDtypeStruct((t, d), F32),
        compiler_params=_cparams(("parallel",)),
        name="ple",
    )(x, p, nw, wg, wp, fn)


def _pack_w_in(w_in):
    offs = [0]
    for s in IN_SIZES:
        offs.append(offs[-1] + s)
    main = w_in[..., offs[0]:offs[7]]
    gates = w_in[..., offs[7]:offs[8]]
    gdn = w_in[..., offs[8]:offs[10]]
    ab = w_in[..., offs[10]:offs[12]]
    per_head = 3 * NSA_GROUP
    zpad = lambda n: jnp.zeros(w_in.shape[:-1] + (n,), w_in.dtype)
    gate_cols = []
    for hk in range(NSA_KV_HEADS):
        gate_cols += [gates[..., hk * per_head:(hk + 1) * per_head], zpad(GATE_ROWS - per_head)]
    tail = zpad(LANES - SMALL_GATE - ab.shape[-1])
    return jnp.concatenate([main, gdn] + gate_cols + [ab, tail], axis=-1).astype(BF16)


def _pack_cmp_w1(w1):
    depth = w1.shape[0]
    w1r = w1.reshape(depth, CMP_BLOCK, NSA_HEAD_DIM, CMP_HIDDEN).astype(BF16)
    z = jnp.zeros_like(w1r)
    return jnp.stack([jnp.concatenate([w1r, z], axis=2), jnp.concatenate([z, w1r], axis=2)], axis=1)


def _rope_tables(seq):
    dim = NSA_HEAD_DIM
    inv = 1.0 / (ROPE_THETA ** (jnp.arange(0, dim, 2, dtype=F32) / dim))
    ang = jnp.arange(seq, dtype=F32)[:, None] * inv[None, :]
    ang = jnp.concatenate([ang, ang], axis=-1)
    cos, sin = jnp.cos(ang), jnp.sin(ang)
    sign = jnp.where(jnp.arange(dim) < dim // 2, -1.0, 1.0).astype(F32)
    return jnp.tile(cos, (1, LANES // dim)), jnp.tile(sin * sign[None, :], (1, LANES // dim))


def _overlap_t(seq, nc_pad):
    n_slc = seq // SLC_BLOCK
    jc = jnp.arange(nc_pad)[None, :]
    js = jnp.arange(LANES)[:, None]
    ov = ((jc * CMP_STRIDE < (js + 1) * SLC_BLOCK) & (jc * CMP_STRIDE + CMP_BLOCK > js * SLC_BLOCK)
          & (js < n_slc))
    return ov.astype(BF16)


def _mask_bands(seq):
    q = (jnp.arange(NSA_GROUP * Q_BLOCK) % Q_BLOCK)[None, :]
    nc = seq // CMP_STRIDE
    rel = jnp.arange(2 * nc)[:, None] - nc
    cband = jnp.where(rel * CMP_STRIDE + (CMP_BLOCK - 1) <= q, 0.0, NEG_INF).astype(F32)
    r = jnp.arange(2 * WINDOW + Q_BLOCK)[:, None]
    wband = jnp.where((q < r) & (r <= q + WINDOW), 0.0, NEG_INF).astype(F32)
    return cband, wband


def _layer(x2, layer, w, consts, *, b, seq, final, cfg):
    t = b * seq
    x2 = _ffn(x2, w['ffn1_norm'], w['ffn1_w1'], w['ffn1_w3'], w['ffn1_w2'], layer,
              tm=cfg['ffn_tm'], tf=cfg['ffn_tf'])
    (q_t, kaug, vs_t, kw, vw_t, gates_t, cmpk, cmpv, qkv, zg, small) = _inproj(
        x2, w['mix_norm'], w['w_in'], consts['cos'], consts['sin'], layer, tm=cfg['in_tm'], b=b, seq=seq)
    kcmp, vcmp_t = _compress(cmpk.reshape(b, seq, LANES), cmpv.reshape(b, seq, LANES), w['cmp_pe'],
                             w['cmp_w1'], w['cmp_w2'], layer)
    o_nsa = _nsa(q_t, kcmp, vcmp_t, kaug, vs_t, kw, vw_t, gates_t, consts['ovt'], consts['cband'],
                 consts['wband'])
    u, wq, kdt, attn, eg = _gdn_prep(qkv.reshape(b, seq, 3 * GDN_WIDTH), w['gdn_conv'],
                                     small.reshape(b, seq, LANES), w['gdn_a_log'], w['gdn_dt_bias'],
                                     layer, ts=cfg['prep_ts'])
    o_gdn = _gdn_scan(u, wq, kdt, attn, eg, zg.reshape(b, seq, GDN_WIDTH), w['gdn_norm'], layer,
                      ts=cfg['scan_ts'])
    return _ffn(x2, w['ffn2_norm'], w['ffn2_w1'], w['ffn2_w3'], w['ffn2_w2'], layer,
                tm=cfg['mix_tm'], tf=cfg['ffn_tf'], final=final,
                mix=(o_nsa.reshape(t, NSA_WIDTH), o_gdn.reshape(t, GDN_WIDTH), w['w_out'], w['p'],
                     w['ple_norm'], w['ple_gate'], w['ple_proj'], consts['final_norm']))


DEFAULT_CFG = dict(ffn_tm=1024, ffn_tf=1408, in_tm=512, prep_ts=256, scan_ts=256,
                   mix_tm=512, ple_tm=512)


def _forward(x, p, w, cfg):
    b, seq, d = x.shape
    depth = p.shape[0]
    t = b * seq
    cos2, sin2 = _rope_tables(seq)
    cband, wband = _mask_bands(seq)
    consts = dict(cos=cos2, sin=sin2, ovt=_overlap_t(seq, seq // CMP_STRIDE), cband=cband, wband=wband,
                  final_norm=w['final_norm'].reshape(1, d))
    bf = lambda a: a.astype(BF16)
    row3 = lambda a: a.reshape(depth, 1, a.shape[-1])
    lane_rows = lambda v, off: jnp.zeros((depth, 1, LANES), F32).at[:, 0, off:off + v.shape[1]].set(v)
    pe = jnp.stack([w['cmp_pe_k'], w['cmp_pe_v']], axis=1)
    w2 = jnp.stack([w['cmp_k_w2'], w['cmp_v_w2']], axis=1)
    ws = dict(
        p=p.reshape(depth, t, p.shape[-1]),
        ffn1_norm=row3(w['ffn1_norm']), ffn1_w1=bf(w['ffn1_w1']), ffn1_w3=bf(w['ffn1_w3']),
        ffn1_w2=bf(w['ffn1_w2']),
        mix_norm=row3(w['mix_norm']), w_in=_pack_w_in(w['w_in']),
        cmp_pe=jnp.concatenate([pe, pe], axis=-1),
        cmp_w1=jnp.stack([_pack_cmp_w1(w['cmp_k_w1']), _pack_cmp_w1(w['cmp_v_w1'])], axis=1),
        cmp_w2=bf(jnp.pad(w2, ((0, 0), (0, 0), (0, 0), (0, LANES - w2.shape[-1])))),
        gdn_conv=w['gdn_conv'], gdn_a_log=lane_rows(w['gdn_a_log'], SMALL_A),
        gdn_dt_bias=lane_rows(w['gdn_dt_bias'], SMALL_A), gdn_norm=row3(w['gdn_norm']),
        w_out=bf(w['w_out']),
        ffn2_norm=row3(w['ffn2_norm']), ffn2_w1=bf(w['ffn2_w1']), ffn2_w3=bf(w['ffn2_w3']),
        ffn2_w2=bf(w['ffn2_w2']),
        ple_norm=row3(w['ple_norm']), ple_gate=bf(w['ple_gate']), ple_proj=bf(w['ple_proj']),
    )
    x2 = x.reshape(t, d)
    for i in range(depth):
        x2 = _layer(x2, i, ws, consts, b=b, seq=seq, final=(i == depth - 1), cfg=cfg)
    return x2.reshape(b, seq, d)


def kernel(x, p, ffn1_norm, ffn1_w1, ffn1_w3, ffn1_w2, mix_norm, w_in, cmp_pe_k, cmp_pe_v,
           cmp_k_w1, cmp_k_w2, cmp_v_w1, cmp_v_w2, gdn_conv, gdn_a_log, gdn_dt_bias, gdn_norm,
           w_out, ffn2_norm, ffn2_w1, ffn2_w3, ffn2_w2, ple_norm, ple_gate, ple_proj, final_norm):
    w = dict(ffn1_norm=ffn1_norm, ffn1_w1=ffn1_w1, ffn1_w3=ffn1_w3, ffn1_w2=ffn1_w2,
             mix_norm=mix_norm, w_in=w_in, cmp_pe_k=cmp_pe_k, cmp_pe_v=cmp_pe_v,
             cmp_k_w1=cmp_k_w1, cmp_k_w2=cmp_k_w2, cmp_v_w1=cmp_v_w1, cmp_v_w2=cmp_v_w2,
             gdn_conv=gdn_conv, gdn_a_log=gdn_a_log, gdn_dt_bias=gdn_dt_bias, gdn_norm=gdn_norm,
             w_out=w_out, ffn2_norm=ffn2_norm, ffn2_w1=ffn2_w1, ffn2_w3=ffn2_w3, ffn2_w2=ffn2_w2,
             ple_norm=ple_norm, ple_gate=ple_gate, ple_proj=ple_proj, final_norm=final_norm)
    return _forward(x, p, w, DEFAULT_CFG)
```
